```python
import math
import jax, jax.numpy as jnp
from jax import lax
import numpy as np

D_MODEL = 1024
BATCH = 32
SEQ = 2048
DEPTH = 4

CHUNK = 64
MIX_WIDTH = D_MODEL
SB_HEADS = 8
SB_HEAD_DIM = 64
SB_WIDTH = SB_HEADS * SB_HEAD_DIM
SB_BLOCK = 128
POOL_WINDOWS = (2, 4, 8, 16)
POOL_GROUPS = len(POOL_WINDOWS)
POOL_GROUP_DIM = 64
POOL_WIDTH = POOL_GROUPS * POOL_GROUP_DIM
CONV_WIDTH = MIX_WIDTH - SB_WIDTH - POOL_WIDTH
CONV_KERNEL = 31
IN_COLS = 3 * SB_WIDTH + POOL_WIDTH + 2 * CONV_WIDTH
FFN_HIDDEN = int(math.ceil((8 * D_MODEL / 3) / 256) * 256)
RMS_EPS = 1e-6
LN_EPS = 1e-5

kernel_name = "hybrid_sb_pool_conformer_trunk"


def _rmsnorm(x, g):
    xf = x.astype(jnp.float32)
    y = xf * lax.rsqrt(jnp.mean(xf * xf, axis=-1, keepdims=True) + RMS_EPS)
    return (y * g.astype(jnp.float32)).astype(x.dtype)


def _layernorm(x, g, b):
    xf = x.astype(jnp.float32)
    mu = jnp.mean(xf, axis=-1, keepdims=True)
    var = jnp.mean(jnp.square(xf - mu), axis=-1, keepdims=True)
    y = (xf - mu) * lax.rsqrt(var + LN_EPS)
    return (y * g.astype(jnp.float32) + b.astype(jnp.float32)).astype(x.dtype)


def _stick_breaking_attention(q, k, v):
    b, t, h, dh = q.shape
    scale = 1.0 / math.sqrt(dh)
    qh = jnp.transpose(q, (0, 2, 1, 3))
    kh = jnp.transpose(k, (0, 2, 1, 3))
    vh = jnp.transpose(v, (0, 2, 1, 3)).astype(jnp.float32)
    outs = []
    for i in range(t // SB_BLOCK):
        q0 = i * SB_BLOCK
        k_end = q0 + SB_BLOCK
        qb = qh[:, :, q0:k_end]
        kb = kh[:, :, :k_end]
        vb = vh[:, :, :k_end]
        z = jnp.einsum('bhqd,bhkd->bhqk', qb, kb).astype(jnp.float32) * scale
        t_idx = q0 + jnp.arange(SB_BLOCK)[:, None]
        s_idx = jnp.arange(k_end)[None, :]
        strict = s_idx < t_idx
        log_beta = jax.nn.log_sigmoid(z)
        log_1mb = jnp.where(strict, jax.nn.log_sigmoid(-z), 0.0)
        suffix = lax.cumsum(log_1mb, axis=3, reverse=True) - log_1mb
        w = jnp.where(strict, jnp.exp(log_beta + suffix), 0.0)
        outs.append(jnp.einsum('bhqk,bhkd->bhqd', w, vb))
    o = jnp.concatenate(outs, axis=2)
    return jnp.transpose(o, (0, 2, 1, 3)).reshape(b, t, h * dh)


def _multiscale_pool(u, pool_w, pool_scale):
    b, t, _ = u.shape
    uf = u.astype(jnp.float32).reshape(b, t, POOL_GROUPS, POOL_GROUP_DIM)
    cs = jnp.cumsum(uf, axis=1)
    pos = jnp.arange(t, dtype=jnp.float32)[None, :, None]
    pooled = []
    for g, w in enumerate(POOL_WINDOWS):
        csg = cs[:, :, g]
        shifted = jnp.pad(csg, ((0, 0), (w, 0), (0, 0)))[:, :t]
        count = jnp.minimum(pos + 1.0, float(w))
        pooled.append((csg - shifted) / count - uf[:, :, g])
    p = jnp.stack(pooled, axis=2)
    y = jnp.einsum('btgc,gcd->btgd', p, pool_w.astype(jnp.float32))
    y = y.reshape(b, t, POOL_WIDTH) * pool_scale.astype(jnp.float32)
    return y.astype(u.dtype)


def _conformer_conv(u, conv_w, conv_b, ln_g, ln_b, pw_out):
    a, gate = jnp.split(u, 2, axis=-1)
    h = a * jax.nn.sigmoid(gate)
    kern = conv_w[:, None, :].astype(h.dtype)
    h = lax.conv_general_dilated(
        h, kern, window_strides=(1,), padding=[(CONV_KERNEL - 1, 0)],
        dimension_numbers=('NWC', 'WIO', 'NWC'), feature_group_count=CONV_WIDTH)
    h = h + conv_b
    h = _layernorm(h, ln_g, ln_b)
    h = jax.nn.silu(h)
    return jnp.einsum('btc,cd->btd', h, pw_out)


def setup_inputs(seed: int = 0) -> dict:
    key = jax.random.key(seed)
    ks = jax.random.split(key, 20)
    f32 = jnp.float32

    def nrm(k, shape, scale):
        return jax.random.normal(k, shape, f32) * scale

    return {
        "x": nrm(ks[0], (BATCH, SEQ, D_MODEL), 1.0),
        "norm_mix_g": 1.0 + nrm(ks[1], (DEPTH, D_MODEL), 0.05),
        "w_in": nrm(ks[2], (DEPTH, D_MODEL, IN_COLS), D_MODEL ** -0.5),
        "sb_q_g": 1.0 + nrm(ks[3], (DEPTH, SB_HEAD_DIM), 0.05),
        "sb_k_g": 1.0 + nrm(ks[4], (DEPTH, SB_HEAD_DIM), 0.05),
        "pool_w": nrm(ks[5], (DEPTH, POOL_GROUPS, POOL_GROUP_DIM, POOL_GROUP_DIM), POOL_GROUP_DIM ** -0.5),
        "pool_scale": 1.0 + nrm(ks[6], (DEPTH, POOL_WIDTH), 0.1),
        "conv_w": nrm(ks[7], (DEPTH, CONV_KERNEL, CONV_WIDTH), CONV_KERNEL ** -0.5),
        "conv_b": nrm(ks[8], (DEPTH, CONV_WIDTH), 0.02),
        "conv_ln_g": 1.0 + nrm(ks[9], (DEPTH, CONV_WIDTH), 0.05),
        "conv_ln_b": nrm(ks[10], (DEPTH, CONV_WIDTH), 0.02),
        "conv_pw": nrm(ks[11], (DEPTH, CONV_WIDTH, CONV_WIDTH), CONV_WIDTH ** -0.5),
        "w_out": nrm(ks[12], (DEPTH, MIX_WIDTH, D_MODEL), MIX_WIDTH ** -0.5),
        "norm_ffn_g": 1.0 + nrm(ks[13], (DEPTH, D_MODEL), 0.05),
        "ffn_w_gu": nrm(ks[14], (DEPTH, D_MODEL, 2 * FFN_HIDDEN), D_MODEL ** -0.5),
        "ffn_w_down": nrm(ks[15], (DEPTH, FFN_HIDDEN, D_MODEL), FFN_HIDDEN ** -0.5),
    }


def reference(x, norm_mix_g, w_in, sb_q_g, sb_k_g, pool_w, pool_scale, conv_w, conv_b,
              conv_ln_g, conv_ln_b, conv_pw, w_out, norm_ffn_g, ffn_w_gu, ffn_w_down):
    b, t, _ = x.shape
    split_pts = [SB_WIDTH, 2 * SB_WIDTH, 3 * SB_WIDTH, 3 * SB_WIDTH + POOL_WIDTH]
    for l in range(DEPTH):
        h = _rmsnorm(x, norm_mix_g[l])
        proj = jnp.einsum('btd,de->bte', h, w_in[l])
        q, k, v, u_pool, u_conv = jnp.split(proj, split_pts, axis=-1)
        q = _rmsnorm(q.reshape(b, t, SB_HEADS, SB_HEAD_DIM).astype(jnp.float32), sb_q_g[l])
        k = _rmsnorm(k.reshape(b, t, SB_HEADS, SB_HEAD_DIM).astype(jnp.float32), sb_k_g[l])
        v = v.reshape(b, t, SB_HEADS, SB_HEAD_DIM)
        a_out = _stick_breaking_attention(q, k, v).astype(x.dtype)
        p_out = _multiscale_pool(u_pool, pool_w[l], pool_scale[l])
        c_out = _conformer_conv(u_conv, conv_w[l], conv_b[l], conv_ln_g[l],
                                conv_ln_b[l], conv_pw[l]).astype(x.dtype)
        mix = jnp.concatenate([a_out, p_out, c_out], axis=-1)
        x = x + jnp.einsum('btm,md->btd', mix, w_out[l])
        h2 = _rmsnorm(x, norm_ffn_g[l])
        gu = jnp.einsum('btd,df->btf', h2, ffn_w_gu[l])
        g, u = jnp.split(gu, 2, axis=-1)
        x = x + jnp.einsum('btf,fd->btd', jax.nn.silu(g) * u, ffn_w_down[l])
    return x
```

```python
import functools
import math

import jax
import jax.numpy as jnp
from jax import lax
from jax.experimental import pallas as pl
from jax.experimental.pallas import tpu as pltpu

F32 = jnp.float32
BF16 = jnp.bfloat16

D_MODEL = 1024
DEPTH = 4
SB_HEADS = 8
SB_HEAD_DIM = 64
SB_WIDTH = SB_HEADS * SB_HEAD_DIM
POOL_WINDOWS = (2, 4, 8, 16)
POOL_GROUP_DIM = 64
POOL_WIDTH = len(POOL_WINDOWS) * POOL_GROUP_DIM
CONV_WIDTH = D_MODEL - SB_WIDTH - POOL_WIDTH
CONV_KERNEL = 31
IN_COLS = 3 * SB_WIDTH + POOL_WIDTH + 2 * CONV_WIDTH
FFN_HIDDEN = int(math.ceil((8 * D_MODEL / 3) / 256) * 256)
RMS_EPS = 1e-6
LN_EPS = 1e-5
SB_SCALE = 1.0 / math.sqrt(SB_HEAD_DIM)

V7X_VMEM_LIMIT_BYTES = 56 * 1024 * 1024
LANES = 128
SUBLANES = 8

TOKEN_TILE = 512
HALO = 32
Q_BLOCK = 256
K_BLOCK = 256
HEADS_PER_STEP = LANES // SB_HEAD_DIM

assert HALO >= CONV_KERNEL - 1 and HALO >= max(POOL_WINDOWS) - 1 and HALO % SUBLANES == 0


def _params(*semantics):
    return pltpu.CompilerParams(dimension_semantics=semantics,
                                vmem_limit_bytes=V7X_VMEM_LIMIT_BYTES)


def _resident(shape):
    return pl.BlockSpec(shape, lambda *_: (0,) * len(shape), pipeline_mode=pl.Buffered(1))


def _rmsnorm(x, g):
    ms = jnp.mean(x * x, axis=-1, keepdims=True)
    return x * lax.rsqrt(ms + RMS_EPS) * g


def _inproj_kernel(x_ref, g_ref, w_ref, qg_ref, kg_ref, headavg_ref,
                   q_ref, k_ref, v_ref, up_ref, uc_ref):
    h = _rmsnorm(x_ref[...], g_ref[...]).astype(BF16)

    def proj(lo, hi):
        return jnp.dot(h, w_ref[:, lo:hi], preferred_element_type=F32)

    def head_norm(t, g):
        ms = jnp.dot((t * t).astype(BF16), headavg_ref[...], preferred_element_type=F32)
        return t * lax.rsqrt(ms + RMS_EPS) * g

    q_ref[...] = (head_norm(proj(0, SB_WIDTH), qg_ref[...]) * SB_SCALE).astype(BF16)
    k_ref[...] = head_norm(proj(SB_WIDTH, 2 * SB_WIDTH), kg_ref[...]).astype(BF16)
    v_ref[...] = proj(2 * SB_WIDTH, 3 * SB_WIDTH).astype(BF16)
    up_ref[...] = proj(3 * SB_WIDTH, 3 * SB_WIDTH + POOL_WIDTH)
    uc_ref[...] = proj(3 * SB_WIDTH + POOL_WIDTH, IN_COLS)


def _inproj(x, g, w, qg, kg, headavg):
    n = x.shape[0]
    tm = TOKEN_TILE
    row = lambda c: pl.BlockSpec((tm, c), lambda i: (i, 0))
    return pl.pallas_call(
        _inproj_kernel,
        grid=(n // tm,),
        in_specs=[row(D_MODEL), _resident((1, D_MODEL)), _resident((D_MODEL, IN_COLS)),
                  _resident((1, SB_WIDTH)), _resident((1, SB_WIDTH)),
                  _resident((SB_WIDTH, SB_WIDTH))],
        out_specs=[row(SB_WIDTH), row(SB_WIDTH), row(SB_WIDTH), row(POOL_WIDTH),
                   row(2 * CONV_WIDTH)],
        out_shape=[jax.ShapeDtypeStruct((n, SB_WIDTH), BF16)] * 3
        + [jax.ShapeDtypeStruct((n, POOL_WIDTH), F32),
           jax.ShapeDtypeStruct((n, 2 * CONV_WIDTH), F32)],
        compiler_params=_params("parallel"),
        name="inproj",
    )(x, g, w, qg, kg, headavg)


def _attn_kernel(q_ref, k_ref, v_ref, tri_ref, o_ref):
    seq = q_ref.shape[0]
    row = lax.broadcasted_iota(jnp.int32, (Q_BLOCK, K_BLOCK), 0)
    col = lax.broadcasted_iota(jnp.int32, (Q_BLOCK, K_BLOCK), 1)
    strict = col < row

    def tile(qh, kt, vt, carry, diagonal):
        z = lax.dot_general(qh, kt, (((1,), (1,)), ((), ())), preferred_element_type=F32)
        softplus_tail = jnp.log(1.0 + jnp.exp(-jnp.abs(z)))
        log_beta = jnp.minimum(z, 0.0) - softplus_tail
        log_1mb = log_beta - z
        if diagonal:
            log_1mb = jnp.where(strict, log_1mb, 0.0)
        suffix = jnp.dot(log_1mb.astype(BF16), tri_ref[...], preferred_element_type=F32)
        w = jnp.exp(log_beta + suffix + carry)
        if diagonal:
            w = jnp.where(strict, w, 0.0)
        pv = jnp.dot(w.astype(BF16), vt, preferred_element_type=F32)
        tile_total = suffix[:, 0:1] + log_1mb[:, 0:1]
        return pv, carry + tile_total

    for h in range(HEADS_PER_STEP):
        lanes = slice(h * SB_HEAD_DIM, (h + 1) * SB_HEAD_DIM)

        def q_block(qi, _, lanes=lanes):
            r0 = pl.multiple_of(qi * Q_BLOCK, Q_BLOCK)
            qh = q_ref[pl.ds(r0, Q_BLOCK), lanes]
            acc, carry = tile(qh, k_ref[pl.ds(r0, K_BLOCK), lanes],
                              v_ref[pl.ds(r0, K_BLOCK), lanes],
                              jnp.zeros((Q_BLOCK, 1), F32), True)

            def k_tile(jj, state):
                acc, carry = state
                c0 = pl.multiple_of((qi - 1 - jj) * K_BLOCK, K_BLOCK)
                pv, carry = tile(qh, k_ref[pl.ds(c0, K_BLOCK), lanes],
                                 v_ref[pl.ds(c0, K_BLOCK), lanes], carry, False)
                return acc + pv, carry

            acc, _ = lax.fori_loop(0, qi, k_tile, (acc, carry))
            o_ref[pl.ds(r0, Q_BLOCK), lanes] = acc.astype(o_ref.dtype)
            return 0

        lax.fori_loop(0, seq // Q_BLOCK, q_block, 0)


def _attention(q, k, v, tri, batch, seq):
    blk = pl.BlockSpec((seq, LANES), lambda b, hp: (b, hp))
    return pl.pallas_call(
        _attn_kernel,
        grid=(batch, SB_WIDTH // LANES),
        in_specs=[blk, blk, blk, _resident((K_BLOCK, K_BLOCK))],
        out_specs=blk,
        out_shape=jax.ShapeDtypeStruct((batch * seq, SB_WIDTH), BF16),
        compiler_params=_params("parallel", "parallel"),
        name="sb_attention",
    )(q, k, v, tri)


def _mix_kernel(a_ref, up_ref, up_halo_ref, uc_ref, uc_halo_ref, x_ref,
                poolw_ref, pscale_ref, convw_ref, convb_ref, lng_ref, lnb_ref, pw_ref,
                wout_ref, o_ref, *, tiles_per_seq):
    tm = up_ref.shape[0]
    rows = tm + HALO
    tile_in_seq = pl.program_id(0) % tiles_per_seq
    has_prev = tile_in_seq > 0

    u = jnp.concatenate([jnp.where(has_prev, up_halo_ref[...], 0.0), up_ref[...]], axis=0)
    s2 = u + pltpu.roll(u, 1, 0)
    s4 = s2 + pltpu.roll(s2, 2, 0)
    s8 = s4 + pltpu.roll(s4, 4, 0)
    s16 = s8 + pltpu.roll(s8, 8, 0)
    group = lax.broadcasted_iota(jnp.int32, (tm, POOL_WIDTH), 1) // POOL_GROUP_DIM
    win_sum = jnp.where(group == 0, s2[HALO:],
                        jnp.where(group == 1, s4[HALO:],
                                  jnp.where(group == 2, s8[HALO:], s16[HALO:])))
    window = jnp.where(group == 0, float(POOL_WINDOWS[0]),
                       jnp.where(group == 1, float(POOL_WINDOWS[1]),
                                 jnp.where(group == 2, float(POOL_WINDOWS[2]),
                                           float(POOL_WINDOWS[3]))))
    pos = (tile_in_seq * tm + lax.broadcasted_iota(jnp.int32, (tm, POOL_WIDTH), 0)).astype(F32)
    pooled = win_sum / jnp.minimum(pos + 1.0, window) - u[HALO:]
    p_out = jnp.dot(pooled.astype(BF16), poolw_ref[...], preferred_element_type=F32)
    p_out = p_out * pscale_ref[...]

    uc = jnp.concatenate([jnp.where(has_prev, uc_halo_ref[...], 0.0), uc_ref[...]], axis=0)
    glu = uc[:, :CONV_WIDTH] * jax.nn.sigmoid(uc[:, CONV_WIDTH:])
    shifted = [glu] + [pltpu.roll(glu, rows - r, 0) for r in range(1, SUBLANES)]
    first_tap = HALO - (CONV_KERNEL - 1)
    conv = None
    for kk in range(CONV_KERNEL):
        r, base = (first_tap + kk) % SUBLANES, (first_tap + kk) // SUBLANES * SUBLANES
        term = shifted[r][base:base + tm] * convw_ref[kk:kk + 1, :]
        conv = term if conv is None else conv + term
    conv = conv + convb_ref[...]
    mu = jnp.mean(conv, axis=-1, keepdims=True)
    cen = conv - mu
    var = jnp.mean(cen * cen, axis=-1, keepdims=True)
    y = cen * lax.rsqrt(var + LN_EPS) * lng_ref[...] + lnb_ref[...]
    y = y * jax.nn.sigmoid(y)
    c_out = jnp.dot(y.astype(BF16), pw_ref[...], preferred_element_type=F32)

    out = x_ref[...]
    out = out + jnp.dot(a_ref[...], wout_ref[0:SB_WIDTH, :], preferred_element_type=F32)
    out = out + jnp.dot(p_out.astype(BF16), wout_ref[SB_WIDTH:SB_WIDTH + POOL_WIDTH, :],
                        preferred_element_type=F32)
    out = out + jnp.dot(c_out.astype(BF16), wout_ref[SB_WIDTH + POOL_WIDTH:, :],
                        preferred_element_type=F32)
    o_ref[...] = out


def _mix(a, up, uc, x, poolw, pscale, convw, convb, lng, lnb, pw, wout, seq):
    n = x.shape[0]
    tm = TOKEN_TILE
    row = lambda c: pl.BlockSpec((tm, c), lambda i: (i, 0))
    halo = lambda c: pl.BlockSpec(
        (HALO, c), lambda i: (jnp.maximum(i * (tm // HALO) - 1, 0), 0))
    return pl.pallas_call(
        functools.partial(_mix_kernel, tiles_per_seq=seq // tm),
        grid=(n // tm,),
        in_specs=[row(SB_WIDTH), row(POOL_WIDTH), halo(POOL_WIDTH), row(2 * CONV_WIDTH),
                  halo(2 * CONV_WIDTH), row(D_MODEL),
                  _resident((POOL_WIDTH, POOL_WIDTH)), _resident((1, POOL_WIDTH)),
                  _resident((CONV_KERNEL, CONV_WIDTH)), _resident((1, CONV_WIDTH)),
                  _resident((1, CONV_WIDTH)), _resident((1, CONV_WIDTH)),
                  _resident((CONV_WIDTH, CONV_WIDTH)), _resident((D_MODEL, D_MODEL))],
        out_specs=row(D_MODEL),
        out_shape=jax.ShapeDtypeStruct((n, D_MODEL), F32),
        compiler_params=_params("parallel"),
        name="mixers_outproj",
    )(a, up, up, uc, uc, x, poolw, pscale, convw, convb, lng, lnb, pw, wout)


def _ffn_kernel(x_ref, g_ref, wgu_ref, wdown_ref, o_ref):
    x = x_ref[...]
    h = _rmsnorm(x, g_ref[...]).astype(BF16)
    gate = jnp.dot(h, wgu_ref[:, :FFN_HIDDEN], preferred_element_type=F32)
    up = jnp.dot(h, wgu_ref[:, FFN_HIDDEN:], preferred_element_type=F32)
    act = (gate * jax.nn.sigmoid(gate) * up).astype(BF16)
    o_ref[...] = x + jnp.dot(act, wdown_ref[...], preferred_element_type=F32)


def _ffn(x, g, wgu, wdown):
    n = x.shape[0]
    tm = TOKEN_TILE
    row = pl.BlockSpec((tm, D_MODEL), lambda i: (i, 0))
    return pl.pallas_call(
        _ffn_kernel,
        grid=(n // tm,),
        in_specs=[row, _resident((1, D_MODEL)), _resident((D_MODEL, 2 * FFN_HIDDEN)),
                  _resident((FFN_HIDDEN, D_MODEL))],
        out_specs=row,
        out_shape=jax.ShapeDtypeStruct((n, D_MODEL), F32),
        compiler_params=_params("parallel"),
        name="swiglu_ffn",
    )(x, g, wgu, wdown)


def kernel(x, norm_mix_g, w_in, sb_q_g, sb_k_g, pool_w, pool_scale, conv_w, conv_b, conv_ln_g,
           conv_ln_b, conv_pw, w_out, norm_ffn_g, ffn_w_gu, ffn_w_down):
    batch, seq, d = x.shape
    assert d == D_MODEL and seq % TOKEN_TILE == 0 and seq % Q_BLOCK == 0
    xf = x.reshape(batch * seq, d)

    head_of = jnp.arange(SB_WIDTH) // SB_HEAD_DIM
    headavg = jnp.where(head_of[:, None] == head_of[None, :], 1.0 / SB_HEAD_DIM, 0.0).astype(BF16)
    kidx = jnp.arange(K_BLOCK)
    tri = (kidx[:, None] > kidx[None, :]).astype(BF16)
    row = lambda v: v.reshape(1, -1).astype(F32)

    for l in range(DEPTH):
        poolw_bd = jax.scipy.linalg.block_diag(*[pool_w[l, g] for g in range(len(POOL_WINDOWS))])
        q, k, v, up, uc = _inproj(
            xf, row(norm_mix_g[l]), w_in[l].astype(BF16),
            row(jnp.tile(sb_q_g[l], SB_HEADS)), row(jnp.tile(sb_k_g[l], SB_HEADS)), headavg)
        a = _attention(q, k, v, tri, batch, seq)
        xf = _mix(a, up, uc, xf, poolw_bd.astype(BF16), row(pool_scale[l]),
                  conv_w[l].astype(F32), row(conv_b[l]), row(conv_ln_g[l]), row(conv_ln_b[l]),
                  conv_pw[l].astype(BF16), w_out[l].astype(BF16), seq)
        xf = _ffn(xf, row(norm_ffn_g[l]), ffn_w_gu[l].astype(BF16), ffn_w_down[l].astype(BF16))
    return xf.reshape(batch, seq, d)
```

```python
import functools
import math

import jax
import jax.numpy as jnp
from jax import lax
from jax.experimental import pallas as pl
from jax.experimental.pallas import tpu as pltpu

F32 = jnp.float32
BF16 = jnp.bfloat16

D_MODEL = 1024
DEPTH = 4
SB_HEADS = 8
SB_HEAD_DIM = 64
SB_WIDTH = SB_HEADS * SB_HEAD_DIM
POOL_WINDOWS = (2, 4, 8, 16)
POOL_GROUP_DIM = 64
POOL_WIDTH = len(POOL_WINDOWS) * POOL_GROUP_DIM
CONV_WIDTH = D_MODEL - SB_WIDTH - POOL_WIDTH
CONV_KERNEL = 31
IN_COLS = 3 * SB_WIDTH + POOL_WIDTH + 2 * CONV_WIDTH
FFN_HIDDEN = int(math.ceil((8 * D_MODEL / 3) / 256) * 256)
RMS_EPS = 1e-6
LN_EPS = 1e-5
LOG2_E = math.log2(math.e)
SB_SCALE_LOG2 = LOG2_E / math.sqrt(SB_HEAD_DIM)

V7X_VMEM_LIMIT_BYTES = 56 * 1024 * 1024
LANES = 128
SUBLANES = 8

TOKEN_TILE = 512
HALO = 32
K_BLOCK = 256
Q_BLOCK = 2 * K_BLOCK

assert HALO >= CONV_KERNEL - 1 and HALO >= max(POOL_WINDOWS) - 1 and HALO % SUBLANES == 0
_NT = (((1,), (1,)), ((), ()))
_NN = (((1,), (0,)), ((), ()))


def _params(*semantics):
    return pltpu.CompilerParams(dimension_semantics=semantics,
                                vmem_limit_bytes=V7X_VMEM_LIMIT_BYTES)


def _resident(shape):
    return pl.BlockSpec(shape, lambda *_: (0,) * len(shape), pipeline_mode=pl.Buffered(1))


def _rmsnorm(x, g):
    ms = jnp.mean(x * x, axis=-1, keepdims=True)
    return x * lax.rsqrt(ms + RMS_EPS) * g


def _inproj_kernel(x_ref, g_ref, w_ref, qg_ref, kg_ref, headavg_ref,
                   q_ref, k_ref, v_ref, up_ref, uc_ref):
    h = _rmsnorm(x_ref[...], g_ref[...]).astype(BF16)

    def proj(lo, hi):
        return jnp.dot(h, w_ref[:, lo:hi], preferred_element_type=F32)

    def head_norm(t, g):
        ms = jnp.dot((t * t).astype(BF16), headavg_ref[...], preferred_element_type=F32)
        return t * lax.rsqrt(ms + RMS_EPS) * g

    q_ref[...] = (head_norm(proj(0, SB_WIDTH), qg_ref[...]) * SB_SCALE_LOG2).astype(BF16)
    k_ref[...] = head_norm(proj(SB_WIDTH, 2 * SB_WIDTH), kg_ref[...]).astype(BF16)
    v_ref[...] = proj(2 * SB_WIDTH, 3 * SB_WIDTH).astype(BF16)
    up_ref[...] = proj(3 * SB_WIDTH, 3 * SB_WIDTH + POOL_WIDTH)
    uc_ref[...] = proj(3 * SB_WIDTH + POOL_WIDTH, IN_COLS)


def _inproj(x, g, w, qg, kg, headavg):
    n = x.shape[0]
    tm = TOKEN_TILE
    row = lambda c: pl.BlockSpec((tm, c), lambda i: (i, 0))
    return pl.pallas_call(
        _inproj_kernel,
        grid=(n // tm,),
        in_specs=[row(D_MODEL), _resident((1, D_MODEL)), _resident((D_MODEL, IN_COLS)),
                  _resident((1, SB_WIDTH)), _resident((1, SB_WIDTH)),
                  _resident((SB_WIDTH, SB_WIDTH))],
        out_specs=[row(SB_WIDTH), row(SB_WIDTH), row(SB_WIDTH), row(POOL_WIDTH),
                   row(2 * CONV_WIDTH)],
        out_shape=[jax.ShapeDtypeStruct((n, SB_WIDTH), BF16)] * 3
        + [jax.ShapeDtypeStruct((n, POOL_WIDTH), F32),
           jax.ShapeDtypeStruct((n, 2 * CONV_WIDTH), F32)],
        compiler_params=_params("parallel"),
        name="inproj",
    )(x, g, w, qg, kg, headavg)


def _attn_kernel(q_ref, k_ref, v_ref, tri_ref, o_ref, acc_ref, carry_ref):
    seq = q_ref.shape[0]
    half = Q_BLOCK // 2
    head_lanes = [slice(h * SB_HEAD_DIM, (h + 1) * SB_HEAD_DIM) for h in range(SB_HEADS)]

    def strict(rows):
        return (lax.broadcasted_iota(jnp.int32, (rows, K_BLOCK), 1)
                < lax.broadcasted_iota(jnp.int32, (rows, K_BLOCK), 0))

    def tile(qh, kt, vt, carry, mask):
        z = lax.dot_general(qh, kt, _NT, preferred_element_type=F32)
        neg_abs = lax.bitcast_convert_type(
            lax.bitcast_convert_type(z, jnp.uint32) | jnp.uint32(0x80000000), F32)
        softplus_tail = jnp.log(1.0 + jnp.exp2(neg_abs)) * LOG2_E
        log_beta = jnp.minimum(z, 0.0) - softplus_tail
        log_1mb = log_beta - z
        if mask is not None:
            log_1mb = jnp.where(mask, log_1mb, 0.0)
        suffix = lax.dot_general(log_1mb, tri_ref[...], _NN, preferred_element_type=F32)
        w = jnp.exp2(log_beta + suffix + carry)
        if mask is not None:
            w = jnp.where(mask, w, 0.0)
        pv = lax.dot_general(w, vt, _NN, preferred_element_type=F32)
        return pv, suffix[:, 0:1] + log_1mb[:, 0:1]

    def q_block(qi, _):
        r0 = pl.multiple_of(qi * Q_BLOCK, Q_BLOCK)
        r1 = pl.multiple_of(qi * Q_BLOCK + half, half)
        for h, lanes in enumerate(head_lanes):
            pv_b, tot_b = tile(q_ref[pl.ds(r1, half), lanes], k_ref[pl.ds(r1, K_BLOCK), lanes],
                               v_ref[pl.ds(r1, K_BLOCK), lanes], 0.0, strict(half))
            carry0 = jnp.concatenate([jnp.zeros((half, 1), F32), tot_b], axis=0)
            pv, tot = tile(q_ref[pl.ds(r0, Q_BLOCK), lanes], k_ref[pl.ds(r0, K_BLOCK), lanes],
                           v_ref[pl.ds(r0, K_BLOCK), lanes], carry0, strict(Q_BLOCK))
            acc_ref[h] = pv + jnp.concatenate([jnp.zeros((half, SB_HEAD_DIM), F32), pv_b], axis=0)
            carry_ref[h] = carry0 + tot

        def k_tile(jj, _):
            c0 = pl.multiple_of((2 * qi - 1 - jj) * K_BLOCK, K_BLOCK)
            for h, lanes in enumerate(head_lanes):
                carry = carry_ref[h]
                pv, tot = tile(q_ref[pl.ds(r0, Q_BLOCK), lanes], k_ref[pl.ds(c0, K_BLOCK), lanes],
                               v_ref[pl.ds(c0, K_BLOCK), lanes], carry, None)
                acc_ref[h] += pv
                carry_ref[h] = carry + tot
            return 0

        lax.fori_loop(0, 2 * qi, k_tile, 0)
        for h, lanes in enumerate(head_lanes):
            o_ref[pl.ds(r0, Q_BLOCK), lanes] = acc_ref[h].astype(o_ref.dtype)
        return 0

    lax.fori_loop(0, seq // Q_BLOCK, q_block, 0)


def _attention(q, k, v, tri, batch, seq):
    blk = pl.BlockSpec((seq, SB_WIDTH), lambda b: (b, 0))
    return pl.pallas_call(
        _attn_kernel,
        grid=(batch,),
        in_specs=[blk, blk, blk, _resident((K_BLOCK, K_BLOCK))],
        out_specs=blk,
        out_shape=jax.ShapeDtypeStruct((batch * seq, SB_WIDTH), BF16),
        scratch_shapes=[pltpu.VMEM((SB_HEADS, Q_BLOCK, SB_HEAD_DIM), F32),
                        pltpu.VMEM((SB_HEADS, Q_BLOCK, 1), F32)],
        compiler_params=_params("parallel"),
        name="sb_attention",
    )(q, k, v, tri)


def _mix_kernel(a_ref, up_ref, up_halo_ref, uc_ref, uc_halo_ref, x_ref,
                poolw_ref, pscale_ref, convw_ref, convb_ref, lng_ref, lnb_ref, pw_ref,
                wout_ref, o_ref, *, tiles_per_seq):
    tm = up_ref.shape[0]
    rows = tm + HALO
    tile_in_seq = pl.program_id(0) % tiles_per_seq
    has_prev = tile_in_seq > 0

    u = jnp.concatenate([jnp.where(has_prev, up_halo_ref[...], 0.0), up_ref[...]], axis=0)
    s2 = u + pltpu.roll(u, 1, 0)
    s4 = s2 + pltpu.roll(s2, 2, 0)
    s8 = s4 + pltpu.roll(s4, 4, 0)
    s16 = s8 + pltpu.roll(s8, 8, 0)
    group = lax.broadcasted_iota(jnp.int32, (tm, POOL_WIDTH), 1) // POOL_GROUP_DIM
    win_sum = jnp.where(group == 0, s2[HALO:],
                        jnp.where(group == 1, s4[HALO:],
                                  jnp.where(group == 2, s8[HALO:], s16[HALO:])))
    window = jnp.where(group == 0, float(POOL_WINDOWS[0]),
                       jnp.where(group == 1, float(POOL_WINDOWS[1]),
                                 jnp.where(group == 2, float(POOL_WINDOWS[2]),
                                           float(POOL_WINDOWS[3]))))
    pos = (tile_in_seq * tm + lax.broadcasted_iota(jnp.int32, (tm, POOL_WIDTH), 0)).astype(F32)
    pooled = win_sum / jnp.minimum(pos + 1.0, window) - u[HALO:]
    p_out = jnp.dot(pooled.astype(BF16), poolw_ref[...], preferred_element_type=F32)
    p_out = p_out * pscale_ref[...]

    uc = jnp.concatenate([jnp.where(has_prev, uc_halo_ref[...], 0.0), uc_ref[...]], axis=0)
    glu = uc[:, :CONV_WIDTH] * jax.nn.sigmoid(uc[:, CONV_WIDTH:])
    shifted = [glu] + [pltpu.roll(glu, rows - r, 0) for r in range(1, SUBLANES)]
    first_tap = HALO - (CONV_KERNEL - 1)
    conv = None
    for kk in range(CONV_KERNEL):
        r, base = (first_tap + kk) % SUBLANES, (first_tap + kk) // SUBLANES * SUBLANES
        term = shifted[r][base:base + tm] * convw_ref[kk:kk + 1, :]
        conv = term if conv is None else conv + term
    conv = conv + convb_ref[...]
    mu = jnp.mean(conv, axis=-1, keepdims=True)
    cen = conv - mu
    var = jnp.mean(cen * cen, axis=-1, keepdims=True)
    y = cen * lax.rsqrt(var + LN_EPS) * lng_ref[...] + lnb_ref[...]
    y = y * jax.nn.sigmoid(y)
    c_out = jnp.dot(y.astype(BF16), pw_ref[...], preferred_element_type=F32)

    out = x_ref[...]
    out = out + jnp.dot(a_ref[...], wout_ref[0:SB_WIDTH, :], preferred_element_type=F32)
    out = out + jnp.dot(p_out.astype(BF16), wout_ref[SB_WIDTH:SB_WIDTH + POOL_WIDTH, :],
                        preferred_element_type=F32)
    out = out + jnp.dot(c_out.astype(BF16), wout_ref[SB_WIDTH + POOL_WIDTH:, :],
                        preferred_element_type=F32)
    o_ref[...] = out


def _mix(a, up, uc, x, poolw, pscale, convw, convb, lng, lnb, pw, wout, seq):
    n = x.shape[0]
    tm = TOKEN_TILE
    row = lambda c: pl.BlockSpec((tm, c), lambda i: (i, 0))
    halo = lambda c: pl.BlockSpec(
        (HALO, c), lambda i: (jnp.maximum(i * (tm // HALO) - 1, 0), 0))
    return pl.pallas_call(
        functools.partial(_mix_kernel, tiles_per_seq=seq // tm),
        grid=(n // tm,),
        in_specs=[row(SB_WIDTH), row(POOL_WIDTH), halo(POOL_WIDTH), row(2 * CONV_WIDTH),
                  halo(2 * CONV_WIDTH), row(D_MODEL),
                  _resident((POOL_WIDTH, POOL_WIDTH)), _resident((1, POOL_WIDTH)),
                  _resident((CONV_KERNEL, CONV_WIDTH)), _resident((1, CONV_WIDTH)),
                  _resident((1, CONV_WIDTH)), _resident((1, CONV_WIDTH)),
                  _resident((CONV_WIDTH, CONV_WIDTH)), _resident((D_MODEL, D_MODEL))],
        out_specs=row(D_MODEL),
        out_shape=jax.ShapeDtypeStruct((n, D_MODEL), F32),
        compiler_params=_params("parallel"),
        name="mixers_outproj",
    )(a, up, up, uc, uc, x, poolw, pscale, convw, convb, lng, lnb, pw, wout)


def _ffn_kernel(x_ref, g_ref, wgu_ref, wdown_ref, o_ref):
    x = x_ref[...]
    h = _rmsnorm(x, g_ref[...]).astype(BF16)
    gate = jnp.dot(h, wgu_ref[:, :FFN_HIDDEN], preferred_element_type=F32)
    up = jnp.dot(h, wgu_ref[:, FFN_HIDDEN:], preferred_element_type=F32)
    act = (gate * jax.nn.sigmoid(gate) * up).astype(BF16)
    o_ref[...] = x + jnp.dot(act, wdown_ref[...], preferred_element_type=F32)


def _ffn(x, g, wgu, wdown):
    n = x.shape[0]
    tm = TOKEN_TILE
    row = pl.BlockSpec((tm, D_MODEL), lambda i: (i, 0))
    return pl.pallas_call(
        _ffn_kernel,
        grid=(n // tm,),
        in_specs=[row, _resident((1, D_MODEL)), _resident((D_MODEL, 2 * FFN_HIDDEN)),
                  _resident((FFN_HIDDEN, D_MODEL))],
        out_specs=row,
        out_shape=jax.ShapeDtypeStruct((n, D_MODEL), F32),
        compiler_params=_params("parallel"),
        name="swiglu_ffn",
    )(x, g, wgu, wdown)


def kernel(x, norm_mix_g, w_in, sb_q_g, sb_k_g, pool_w, pool_scale, conv_w, conv_b, conv_ln_g,
           conv_ln_b, conv_pw, w_out, norm_ffn_g, ffn_w_gu, ffn_w_down):
    batch, seq, d = x.shape
    assert d == D_MODEL and seq % TOKEN_TILE == 0 and seq % Q_BLOCK == 0
    xf = x.reshape(batch * seq, d)

    head_of = jnp.arange(SB_WIDTH) // SB_HEAD_DIM
    headavg = jnp.where(head_of[:, None] == head_of[None, :], 1.0 / SB_HEAD_DIM, 0.0).astype(BF16)
    kidx = jnp.arange(K_BLOCK)
    tri = (kidx[:, None] > kidx[None, :]).astype(BF16)
    row = lambda v: v.reshape(1, -1).astype(F32)

    for l in range(DEPTH):
        poolw_bd = jax.scipy.linalg.block_diag(*[pool_w[l, g] for g in range(len(POOL_WINDOWS))])
        q, k, v, up, uc = _inproj(
            xf, row(norm_mix_g[l]), w_in[l].astype(BF16),
            row(jnp.tile(sb_q_g[l], SB_HEADS)), row(jnp.tile(sb_k_g[l], SB_HEADS)), headavg)
        a = _attention(q, k, v, tri, batch, seq)
        xf = _mix(a, up, uc, xf, poolw_bd.astype(BF16), row(pool_scale[l]),
                  conv_w[l].astype(F32), row(conv_b[l]), row(conv_ln_g[l]), row(conv_ln_b[l]),
                  conv_pw[l].astype(BF16), w_out[l].astype(BF16), seq)
        xf = _ffn(xf, row(norm_ffn_g[l]), ffn_w_gu[l].astype(BF16), ffn_w_down[l].astype(BF16))
    return xf.reshape(batch, seq, d)
```

```python
import functools
import math

import jax
import jax.numpy as jnp
from jax import lax
from jax.experimental import pallas as pl
from jax.experimental.pallas import tpu as pltpu

F32 = jnp.float32
BF16 = jnp.bfloat16

D_MODEL = 1024
DEPTH = 4
SB_HEADS = 8
SB_HEAD_DIM = 64
SB_WIDTH = SB_HEADS * SB_HEAD_DIM
POOL_WINDOWS = (2, 4, 8, 16)
POOL_GROUP_DIM = 64
POOL_WIDTH = len(POOL_WINDOWS) * POOL_GROUP_DIM
CONV_WIDTH = D_MODEL - SB_WIDTH - POOL_WIDTH
CONV_KERNEL = 31
IN_COLS = 3 * SB_WIDTH + POOL_WIDTH + 2 * CONV_WIDTH
FFN_HIDDEN = int(math.ceil((8 * D_MODEL / 3) / 256) * 256)
RMS_EPS = 1e-6
LN_EPS = 1e-5
LOG2_E = math.log2(math.e)
SB_SCALE_LOG2 = LOG2_E / math.sqrt(SB_HEAD_DIM)
F32_EXP2_ZERO_BELOW = -152.0
LOGIT_BOUND_MARGIN = 1.05

V7X_VMEM_LIMIT_BYTES = 56 * 1024 * 1024
LANES = 128
SUBLANES = 8
MXU_TILE = 256

TOKEN_TILE = 512
HALO = 32
K_BLOCK = MXU_TILE
Q_BLOCK = 2 * K_BLOCK
CONV_ROW_BLOCK = 128
FFN_CHUNKS = ((0, 768), (768, 1536), (1536, 2304), (2304, FFN_HIDDEN))

assert HALO >= CONV_KERNEL - 1 and HALO >= max(POOL_WINDOWS) - 1 and HALO % SUBLANES == 0
assert all(lo % MXU_TILE == 0 and hi % MXU_TILE == 0 for lo, hi in FFN_CHUNKS)
assert TOKEN_TILE // CONV_ROW_BLOCK == len(FFN_CHUNKS)
_NT = (((1,), (1,)), ((), ()))
_NN = (((1,), (0,)), ((), ()))


def _params(*semantics):
    return pltpu.CompilerParams(dimension_semantics=semantics,
                                vmem_limit_bytes=V7X_VMEM_LIMIT_BYTES)


def _resident(shape):
    return pl.BlockSpec(shape, lambda *_: (0,) * len(shape), pipeline_mode=pl.Buffered(1))


def _rmsnorm(x, g):
    ms = jnp.mean(x * x, axis=-1, keepdims=True)
    return x * lax.rsqrt(ms + RMS_EPS) * g


def _inproj_kernel(x_ref, g_ref, w_ref, qg_ref, kg_ref, headavg_ref,
                   q_ref, k_ref, v_ref, up_ref, uc_ref):
    h = _rmsnorm(x_ref[...], g_ref[...]).astype(BF16)

    def proj(lo, hi):
        return jnp.dot(h, w_ref[:, lo:hi], preferred_element_type=F32)

    def head_norm(t, g):
        sq = (t * t).astype(BF16)
        ms = jnp.concatenate(
            [jnp.dot(sq[:, c:c + MXU_TILE], headavg_ref[...], preferred_element_type=F32)
             for c in range(0, SB_WIDTH, MXU_TILE)], axis=-1)
        return t * lax.rsqrt(ms + RMS_EPS) * g

    q_ref[...] = (head_norm(proj(0, SB_WIDTH), qg_ref[...]) * SB_SCALE_LOG2).astype(BF16)
    k_ref[...] = head_norm(proj(SB_WIDTH, 2 * SB_WIDTH), kg_ref[...]).astype(BF16)
    v_ref[...] = proj(2 * SB_WIDTH, 3 * SB_WIDTH).astype(BF16)
    up_ref[...] = proj(3 * SB_WIDTH, 3 * SB_WIDTH + POOL_WIDTH)
    uc_ref[...] = proj(3 * SB_WIDTH + POOL_WIDTH, IN_COLS)


def _inproj(x, g, w, qg, kg, headavg):
    n = x.shape[0]
    tm = TOKEN_TILE
    row = lambda c: pl.BlockSpec((tm, c), lambda i: (i, 0))
    return pl.pallas_call(
        _inproj_kernel,
        grid=(n // tm,),
        in_specs=[row(D_MODEL), _resident((1, D_MODEL)), _resident((D_MODEL, IN_COLS)),
                  _resident((1, SB_WIDTH)), _resident((1, SB_WIDTH)),
                  _resident((MXU_TILE, MXU_TILE))],
        out_specs=[row(SB_WIDTH), row(SB_WIDTH), row(SB_WIDTH), row(POOL_WIDTH),
                   row(2 * CONV_WIDTH)],
        out_shape=[jax.ShapeDtypeStruct((n, SB_WIDTH), BF16)] * 3
        + [jax.ShapeDtypeStruct((n, POOL_WIDTH), F32),
           jax.ShapeDtypeStruct((n, 2 * CONV_WIDTH), F32)],
        compiler_params=_params("parallel"),
        name="inproj",
    )(x, g, w, qg, kg, headavg)


def _attn_kernel(skip_below_ref, q_ref, k_ref, v_ref, tri_ref, o_ref, acc_ref, carry_ref):
    seq = q_ref.shape[0]
    half = Q_BLOCK // 2
    head_lanes = [slice(h * SB_HEAD_DIM, (h + 1) * SB_HEAD_DIM) for h in range(SB_HEADS)]
    pair_lanes = [slice(h // 2 * LANES, (h // 2 + 1) * LANES) for h in range(SB_HEADS)]

    def strict(rows):
        return (lax.broadcasted_iota(jnp.int32, (rows, K_BLOCK), 1)
                < lax.broadcasted_iota(jnp.int32, (rows, K_BLOCK), 0))

    def lane_bcast(col):
        return jnp.broadcast_to(col, (col.shape[0], LANES))

    def tile(qh, kt, vt, carry, mask):
        z = lax.dot_general(qh, kt, _NT, preferred_element_type=F32)
        neg_abs = lax.bitcast_convert_type(
            lax.bitcast_convert_type(z, jnp.uint32) | jnp.uint32(0x80000000), F32)
        softplus = jnp.maximum(z, 0.0) + jnp.log(1.0 + jnp.exp2(neg_abs)) * LOG2_E
        if mask is not None:
            softplus = jnp.where(mask, softplus, 0.0)
        incl = lax.dot_general(softplus, tri_ref[...], _NN, preferred_element_type=F32)
        expo = z + incl
        if carry is not None:
            expo = expo + jnp.concatenate([carry] * (K_BLOCK // LANES), axis=1)
        w = jnp.exp2(expo)
        if mask is not None:
            w = jnp.where(mask, w, 0.0)
        pv = lax.dot_general(w, vt, _NN, preferred_element_type=F32)
        if mask is None:
            new_carry = lane_bcast(expo[:, 0:1] - z[:, 0:1])
        else:
            new_carry = lane_bcast(incl[:, 0:1])
            if carry is not None:
                new_carry = new_carry + carry
        return pv, new_carry

    def q_block(qi, _):
        r0 = pl.multiple_of(qi * Q_BLOCK, Q_BLOCK)
        r1 = pl.multiple_of(qi * Q_BLOCK + half, half)
        for h, (lanes, pair) in enumerate(zip(head_lanes, pair_lanes)):
            pv_b, carry_b = tile(q_ref[pl.ds(r1, half), lanes], k_ref[pl.ds(r1, K_BLOCK), lanes],
                                 v_ref[pl.ds(r1, K_BLOCK), pair], None, strict(half))
            carry0 = jnp.concatenate([jnp.zeros((half, LANES), F32), carry_b], axis=0)
            pv, carry = tile(q_ref[pl.ds(r0, Q_BLOCK), lanes], k_ref[pl.ds(r0, K_BLOCK), lanes],
                             v_ref[pl.ds(r0, K_BLOCK), pair], carry0, strict(Q_BLOCK))
            acc_ref[h] = pv + jnp.concatenate([jnp.zeros((half, LANES), F32), pv_b], axis=0)
            carry_ref[h] = carry

        def more_tiles(state):
            jj, carry_max = state
            return jnp.logical_and(jj < 2 * qi, carry_max > skip_below_ref[0])

        def k_tile(state):
            jj, _ = state
            c0 = pl.multiple_of((2 * qi - 1 - jj) * K_BLOCK, K_BLOCK)
            carry_max = None
            for h, (lanes, pair) in enumerate(zip(head_lanes, pair_lanes)):
                pv, carry = tile(q_ref[pl.ds(r0, Q_BLOCK), lanes], k_ref[pl.ds(c0, K_BLOCK), lanes],
                                 v_ref[pl.ds(c0, K_BLOCK), pair], carry_ref[h], None)
                acc_ref[h] += pv
                carry_ref[h] = carry
                carry_max = carry if carry_max is None else jnp.maximum(carry_max, carry)
            return jj + 1, jnp.max(carry_max)

        lax.while_loop(more_tiles, k_tile, (jnp.int32(0), jnp.float32(0.0)))
        first_of_pair = lax.broadcasted_iota(jnp.int32, (Q_BLOCK, LANES), 1) < SB_HEAD_DIM
        for p in range(SB_HEADS // 2):
            both = jnp.where(first_of_pair, acc_ref[2 * p], acc_ref[2 * p + 1])
            o_ref[pl.ds(r0, Q_BLOCK), p * LANES:(p + 1) * LANES] = both.astype(o_ref.dtype)
        return 0

    lax.fori_loop(0, seq // Q_BLOCK, q_block, 0)


def _attention_skip_threshold(q_gain, k_gain):
    logit_bound = (SB_HEAD_DIM * SB_SCALE_LOG2 * LOGIT_BOUND_MARGIN
                   * jnp.max(jnp.abs(q_gain)) * jnp.max(jnp.abs(k_gain)))
    return (F32_EXP2_ZERO_BELOW - logit_bound).reshape(1).astype(F32)


def _attention(skip_below, q, k, v, tri, batch, seq):
    blk = pl.BlockSpec((seq, SB_WIDTH), lambda b: (b, 0))
    return pl.pallas_call(
        _attn_kernel,
        grid=(batch,),
        in_specs=[pl.BlockSpec(memory_space=pltpu.SMEM), blk, blk, blk,
                  _resident((K_BLOCK, K_BLOCK))],
        out_specs=blk,
        out_shape=jax.ShapeDtypeStruct((batch * seq, SB_WIDTH), BF16),
        scratch_shapes=[pltpu.VMEM((SB_HEADS, Q_BLOCK, LANES), F32),
                        pltpu.VMEM((SB_HEADS, Q_BLOCK, LANES), F32)],
        compiler_params=_params("parallel"),
        name="sb_attention",
    )(skip_below, q, k, v, tri)


def _zero_after(v):
    bits = lax.bitcast_convert_type(v[0:1, 0:1], jnp.uint32)
    return ((bits >> 16) >> 16).astype(F32)


def _mixers_steps(a_ref, up_ref, up_halo_ref, uc_ref, uc_halo_ref, x_ref,
                  poolw_ref, pscale_ref, convw_ref, convb_ref, lng_ref, lnb_ref, pw_ref,
                  wout_ref, x1_ref, tile_in_seq, anchors):
    tm = up_ref.shape[0]
    rows = tm + HALO
    has_prev = tile_in_seq > 0

    u = jnp.concatenate([jnp.where(has_prev, up_halo_ref[...], 0.0), up_ref[...]], axis=0)
    s2 = u + pltpu.roll(u, 1, 0)
    s4 = s2 + pltpu.roll(s2, 2, 0)
    s8 = s4 + pltpu.roll(s4, 4, 0)
    s16 = s8 + pltpu.roll(s8, 8, 0)
    group = lax.broadcasted_iota(jnp.int32, (tm, POOL_WIDTH), 1) // POOL_GROUP_DIM
    win_sum = jnp.where(group == 0, s2[HALO:],
                        jnp.where(group == 1, s4[HALO:],
                                  jnp.where(group == 2, s8[HALO:], s16[HALO:])))
    window = jnp.where(group == 0, float(POOL_WINDOWS[0]),
                       jnp.where(group == 1, float(POOL_WINDOWS[1]),
                                 jnp.where(group == 2, float(POOL_WINDOWS[2]),
                                           float(POOL_WINDOWS[3]))))
    pos = (tile_in_seq * tm + lax.broadcasted_iota(jnp.int32, (tm, POOL_WIDTH), 0)).astype(F32)
    pooled = win_sum / jnp.minimum(pos + 1.0, window) - u[HALO:]
    p_out = jnp.dot(pooled.astype(BF16), poolw_ref[...], preferred_element_type=F32)
    p_out = (p_out * pscale_ref[...]).astype(BF16)
    yield

    uc = jnp.concatenate([jnp.where(has_prev, uc_halo_ref[...], 0.0), uc_ref[...]], axis=0)
    glu = uc[:, :CONV_WIDTH] * jax.nn.sigmoid(uc[:, CONV_WIDTH:])
    shifted = [glu] + [pltpu.roll(glu, rows - r, 0) for r in range(1, SUBLANES)]
    yield
    first_tap = HALO - (CONV_KERNEL - 1)
    ys = []
    for r0 in range(0, tm, CONV_ROW_BLOCK):
        conv = None
        for kk in range(CONV_KERNEL):
            r, base = (first_tap + kk) % SUBLANES, (first_tap + kk) // SUBLANES * SUBLANES
            term = shifted[r][base + r0:base + r0 + CONV_ROW_BLOCK] * convw_ref[kk:kk + 1, :]
            conv = term if conv is None else conv + term
        conv = conv + convb_ref[...]
        mu = jnp.mean(conv, axis=-1, keepdims=True)
        cen = conv - mu
        var = jnp.mean(cen * cen, axis=-1, keepdims=True)
        y = cen * lax.rsqrt(var + LN_EPS) * lng_ref[...] + lnb_ref[...]
        y = y * jax.nn.sigmoid(y)
        anchors.append(_zero_after(y))
        ys.append(y.astype(BF16))
        yield
    c_out = jnp.dot(jnp.concatenate(ys, axis=0), pw_ref[...], preferred_element_type=F32)

    mix = jnp.concatenate([a_ref[...], p_out, c_out.astype(BF16)], axis=-1)
    x1_ref[...] = x_ref[...] + jnp.dot(mix, wout_ref[...], preferred_element_type=F32)
    yield


def _ffn_steps(x1, g_ref, wgu_ref, wdown_ref, o_ref, anchors):
    h = _rmsnorm(x1, g_ref[...]).astype(BF16)
    yield
    acc = x1
    for c, (lo, hi) in enumerate(FFN_CHUNKS):
        hc = h if c == 0 else h + anchors[c - 1].astype(BF16)
        gate = jnp.dot(hc, wgu_ref[:, lo:hi], preferred_element_type=F32)
        up = jnp.dot(hc, wgu_ref[:, FFN_HIDDEN + lo:FFN_HIDDEN + hi], preferred_element_type=F32)
        act = (gate * jax.nn.sigmoid(gate) * up).astype(BF16)
        acc = acc + jnp.dot(act, wdown_ref[lo:hi, :], preferred_element_type=F32)
        yield
    o_ref[...] = acc
    yield


def _mix_ffn_kernel(a_ref, up_ref, up_halo_ref, uc_ref, uc_halo_ref, x_ref,
                    poolw_ref, pscale_ref, convw_ref, convb_ref, lng_ref, lnb_ref, pw_ref,
                    wout_ref, g_ref, wgu_ref, wdown_ref, o_ref, x1_ref, *, tiles_per_seq, n_tiles):
    i = pl.program_id(0)

    @pl.when(i == 0)
    def _():
        x1_ref[...] = jnp.zeros_like(x1_ref)

    tile = jnp.minimum(i, n_tiles - 1)
    anchors = []
    ffn = _ffn_steps(x1_ref[...], g_ref, wgu_ref, wdown_ref, o_ref, anchors)
    mixers = _mixers_steps(a_ref, up_ref, up_halo_ref, uc_ref, uc_halo_ref, x_ref,
                           poolw_ref, pscale_ref, convw_ref, convb_ref, lng_ref, lnb_ref,
                           pw_ref, wout_ref, x1_ref, tile % tiles_per_seq, anchors)
    next(ffn)
    next(mixers)
    next(mixers)
    for _ in FFN_CHUNKS:
        next(ffn)
        next(mixers)
    next(ffn)
    next(mixers)


def _mix_ffn(a, up, uc, x, poolw, pscale, convw, convb, lng, lnb, pw, wout, g, wgu, wdown, seq):
    n = x.shape[0]
    tm = TOKEN_TILE
    nt = n // tm
    cur = lambda i: jnp.minimum(i, nt - 1)
    row = lambda c: pl.BlockSpec((tm, c), lambda i: (cur(i), 0))
    halo = lambda c: pl.BlockSpec(
        (HALO, c), lambda i: (jnp.maximum(cur(i) * (tm // HALO) - 1, 0), 0))
    return pl.pallas_call(
        functools.partial(_mix_ffn_kernel, tiles_per_seq=seq // tm, n_tiles=nt),
        grid=(nt + 1,),
        in_specs=[row(SB_WIDTH), row(POOL_WIDTH), halo(POOL_WIDTH), row(2 * CONV_WIDTH),
                  halo(2 * CONV_WIDTH), row(D_MODEL),
                  _resident((POOL_WIDTH, POOL_WIDTH)), _resident((1, POOL_WIDTH)),
                  _resident((CONV_KERNEL, CONV_WIDTH)), _resident((1, CONV_WIDTH)),
                  _resident((1, CONV_WIDTH)), _resident((1, CONV_WIDTH)),
                  _resident((CONV_WIDTH, CONV_WIDTH)), _resident((D_MODEL, D_MODEL)),
                  _resident((1, D_MODEL)), _resident((D_MODEL, 2 * FFN_HIDDEN)),
                  _resident((FFN_HIDDEN, D_MODEL))],
        out_specs=pl.BlockSpec((tm, D_MODEL), lambda i: (jnp.maximum(i - 1, 0), 0)),
        out_shape=jax.ShapeDtypeStruct((n, D_MODEL), F32),
        scratch_shapes=[pltpu.VMEM((tm, D_MODEL), F32)],
        compiler_params=_params("arbitrary"),
        name="mixers_ffn",
    )(a, up, up, uc, uc, x, poolw, pscale, convw, convb, lng, lnb, pw, wout, g, wgu, wdown)


def kernel(x, norm_mix_g, w_in, sb_q_g, sb_k_g, pool_w, pool_scale, conv_w, conv_b, conv_ln_g,
           conv_ln_b, conv_pw, w_out, norm_ffn_g, ffn_w_gu, ffn_w_down):
    batch, seq, d = x.shape
    assert d == D_MODEL and seq % TOKEN_TILE == 0 and seq % Q_BLOCK == 0
    xf = x.reshape(batch * seq, d)

    head_of = jnp.arange(MXU_TILE) // SB_HEAD_DIM
    headavg = jnp.where(head_of[:, None] == head_of[None, :], 1.0 / SB_HEAD_DIM, 0.0).astype(BF16)
    kidx = jnp.arange(K_BLOCK)
    tri = -(kidx[:, None] >= kidx[None, :]).astype(BF16)
    row = lambda v: v.reshape(1, -1).astype(F32)

    for l in range(DEPTH):
        poolw_bd = jax.scipy.linalg.block_diag(*[pool_w[l, g] for g in range(len(POOL_WINDOWS))])
        q, k, v, up, uc = _inproj(
            xf, row(norm_mix_g[l]), w_in[l].astype(BF16),
            row(jnp.tile(sb_q_g[l], SB_HEADS)), row(jnp.tile(sb_k_g[l], SB_HEADS)), headavg)
        a = _attention(_attention_skip_threshold(sb_q_g[l], sb_k_g[l]), q, k, v, tri, batch, seq)
        xf = _mix_ffn(a, up, uc, xf, poolw_bd.astype(BF16), row(pool_scale[l]),
                      conv_w[l].astype(F32), row(conv_b[l]), row(conv_ln_g[l]),
                      row(conv_ln_b[l]), conv_pw[l].astype(BF16), w_out[l].astype(BF16),
                      row(norm_ffn_g[l]), ffn_w_gu[l].astype(BF16), ffn_w_down[l].astype(BF16),
                      seq)
    return xf.reshape(batch, seq, d)
```

```python
import functools
import math

import jax
import jax.numpy as jnp
from jax import lax
from jax.experimental import pallas as pl
from jax.experimental.pallas import tpu as pltpu

F32 = jnp.float32
BF16 = jnp.bfloat16

D_MODEL = 1024
DEPTH = 4
SB_HEADS = 8
SB_HEAD_DIM = 64
SB_WIDTH = SB_HEADS * SB_HEAD_DIM
POOL_WINDOWS = (2, 4, 8, 16)
POOL_GROUP_DIM = 64
POOL_WIDTH = len(POOL_WINDOWS) * POOL_GROUP_DIM
CONV_WIDTH = D_MODEL - SB_WIDTH - POOL_WIDTH
CONV_KERNEL = 31
IN_COLS = 3 * SB_WIDTH + POOL_WIDTH + 2 * CONV_WIDTH
FFN_HIDDEN = int(math.ceil((8 * D_MODEL / 3) / 256) * 256)
RMS_EPS = 1e-6
LN_EPS = 1e-5
LOG2_E = math.log2(math.e)
SB_SCALE_LOG2 = LOG2_E / math.sqrt(SB_HEAD_DIM)
F32_EXP2_ZERO_BELOW = -152.0
LOGIT_BOUND_MARGIN = 1.05

V7X_VMEM_LIMIT_BYTES = 56 * 1024 * 1024
LANES = 128
SUBLANES = 8
MXU_TILE = 256

TOKEN_TILE = 512
HALO = 32
K_BLOCK = MXU_TILE
Q_BLOCK = 2 * K_BLOCK
CONV_ROW_BLOCK = 128
FFN_CHUNKS = ((0, 768), (768, 1536), (1536, 2304), (2304, FFN_HIDDEN))

assert HALO >= CONV_KERNEL - 1 and HALO >= max(POOL_WINDOWS) - 1 and HALO % SUBLANES == 0
assert all(lo % MXU_TILE == 0 and hi % MXU_TILE == 0 for lo, hi in FFN_CHUNKS)
CONV_PIECES_PER_CHUNK = 2
assert TOKEN_TILE // CONV_ROW_BLOCK == CONV_PIECES_PER_CHUNK * (len(FFN_CHUNKS) - 2)
_NT = (((1,), (1,)), ((), ()))
_NN = (((1,), (0,)), ((), ()))


def _params(*semantics):
    return pltpu.CompilerParams(dimension_semantics=semantics,
                                vmem_limit_bytes=V7X_VMEM_LIMIT_BYTES)


def _resident(shape):
    return pl.BlockSpec(shape, lambda *_: (0,) * len(shape), pipeline_mode=pl.Buffered(1))


def _rmsnorm(x, g):
    ms = jnp.mean(x * x, axis=-1, keepdims=True)
    return x * lax.rsqrt(ms + RMS_EPS) * g


def _inproj_kernel(x_ref, g_ref, w_ref, qg_ref, kg_ref, headavg_ref,
                   q_ref, k_ref, v_ref, up_ref, uc_ref):
    h = _rmsnorm(x_ref[...], g_ref[...]).astype(BF16)

    def proj(lo, hi):
        return jnp.dot(h, w_ref[:, lo:hi], preferred_element_type=F32)

    def head_norm(t, g):
        sq = (t * t).astype(BF16)
        ms = jnp.concatenate(
            [jnp.dot(sq[:, c:c + MXU_TILE], headavg_ref[...], preferred_element_type=F32)
             for c in range(0, SB_WIDTH, MXU_TILE)], axis=-1)
        return t * lax.rsqrt(ms + RMS_EPS) * g

    q_ref[...] = (head_norm(proj(0, SB_WIDTH), qg_ref[...]) * SB_SCALE_LOG2).astype(BF16)
    k_ref[...] = head_norm(proj(SB_WIDTH, 2 * SB_WIDTH), kg_ref[...]).astype(BF16)
    v_ref[...] = proj(2 * SB_WIDTH, 3 * SB_WIDTH).astype(BF16)
    up_ref[...] = proj(3 * SB_WIDTH, 3 * SB_WIDTH + POOL_WIDTH)
    uc_ref[...] = proj(3 * SB_WIDTH + POOL_WIDTH, IN_COLS)


def _inproj(x, g, w, qg, kg, headavg):
    n = x.shape[0]
    tm = TOKEN_TILE
    row = lambda c: pl.BlockSpec((tm, c), lambda i: (i, 0))
    return pl.pallas_call(
        _inproj_kernel,
        grid=(n // tm,),
        in_specs=[row(D_MODEL), _resident((1, D_MODEL)), _resident((D_MODEL, IN_COLS)),
                  _resident((1, SB_WIDTH)), _resident((1, SB_WIDTH)),
                  _resident((MXU_TILE, MXU_TILE))],
        out_specs=[row(SB_WIDTH), row(SB_WIDTH), row(SB_WIDTH), row(POOL_WIDTH),
                   row(2 * CONV_WIDTH)],
        out_shape=[jax.ShapeDtypeStruct((n, SB_WIDTH), BF16)] * 3
        + [jax.ShapeDtypeStruct((n, POOL_WIDTH), F32),
           jax.ShapeDtypeStruct((n, 2 * CONV_WIDTH), F32)],
        compiler_params=_params("parallel"),
        name="inproj",
    )(x, g, w, qg, kg, headavg)


def _attn_kernel(skip_below_ref, q_ref, k_ref, v_ref, tri_ref, o_ref, acc_ref, carry_ref):
    seq = q_ref.shape[0]
    half = Q_BLOCK // 2
    head_lanes = [slice(h * SB_HEAD_DIM, (h + 1) * SB_HEAD_DIM) for h in range(SB_HEADS)]
    pair_lanes = [slice(p * LANES, (p + 1) * LANES) for p in range(SB_HEADS // 2)]
    lane_is_first = lax.broadcasted_iota(jnp.int32, (K_BLOCK, LANES), 1) < SB_HEAD_DIM

    def strict(rows):
        return (lax.broadcasted_iota(jnp.int32, (rows, K_BLOCK), 1)
                < lax.broadcasted_iota(jnp.int32, (rows, K_BLOCK), 0))

    def lane_bcast(col):
        return jnp.broadcast_to(col, (col.shape[0], LANES))

    def tile(qh, kt, carry, mask):
        z = lax.dot_general(qh, kt, _NT, preferred_element_type=F32)
        neg_abs = lax.bitcast_convert_type(
            lax.bitcast_convert_type(z, jnp.uint32) | jnp.uint32(0x80000000), F32)
        softplus = jnp.maximum(z, 0.0) + jnp.log(1.0 + jnp.exp2(neg_abs)) * LOG2_E
        if mask is not None:
            softplus = jnp.where(mask, softplus, 0.0)
        incl = lax.dot_general(softplus, tri_ref[...], _NN, preferred_element_type=F32)
        expo = z + incl
        if carry is not None:
            expo = expo + jnp.concatenate([carry] * (K_BLOCK // LANES), axis=1)
        w = jnp.exp2(expo)
        if mask is not None:
            w = jnp.where(mask, w, 0.0)
        if mask is None:
            new_carry = lane_bcast(expo[:, 0:1] - z[:, 0:1])
        else:
            new_carry = lane_bcast(incl[:, 0:1])
            if carry is not None:
                new_carry = new_carry + carry
        return w, new_carry

    def pair_pv(w_first, w_second, v_pair):
        zero = jnp.zeros_like(v_pair)
        return (lax.dot_general(w_first, jnp.where(lane_is_first, v_pair, zero), _NN,
                                preferred_element_type=F32)
                + lax.dot_general(w_second, jnp.where(lane_is_first, zero, v_pair), _NN,
                                  preferred_element_type=F32))

    def q_block(qi, _):
        r0 = pl.multiple_of(qi * Q_BLOCK, Q_BLOCK)
        r1 = pl.multiple_of(qi * Q_BLOCK + half, half)
        for p, pair in enumerate(pair_lanes):
            ws_b, ws = [], []
            for h in (2 * p, 2 * p + 1):
                lanes = head_lanes[h]
                w_b, carry_b = tile(q_ref[pl.ds(r1, half), lanes],
                                    k_ref[pl.ds(r1, K_BLOCK), lanes], None, strict(half))
                carry0 = jnp.concatenate([jnp.zeros((half, LANES), F32), carry_b], axis=0)
                w, carry = tile(q_ref[pl.ds(r0, Q_BLOCK), lanes],
                                k_ref[pl.ds(r0, K_BLOCK), lanes], carry0, strict(Q_BLOCK))
                carry_ref[h] = carry
                ws_b.append(w_b)
                ws.append(w)
            pv_b = pair_pv(ws_b[0], ws_b[1], v_ref[pl.ds(r1, K_BLOCK), pair])
            pv = pair_pv(ws[0], ws[1], v_ref[pl.ds(r0, K_BLOCK), pair])
            acc_ref[p] = pv + jnp.concatenate([jnp.zeros((half, LANES), F32), pv_b], axis=0)

        def more_tiles(state):
            jj, carry_max = state
            return jnp.logical_and(jj < 2 * qi, carry_max > skip_below_ref[0])

        def k_tile(state):
            jj, _ = state
            c0 = pl.multiple_of((2 * qi - 1 - jj) * K_BLOCK, K_BLOCK)
            carry_max = None
            for p, pair in enumerate(pair_lanes):
                ws = []
                for h in (2 * p, 2 * p + 1):
                    w, carry = tile(q_ref[pl.ds(r0, Q_BLOCK), head_lanes[h]],
                                    k_ref[pl.ds(c0, K_BLOCK), head_lanes[h]], carry_ref[h], None)
                    carry_ref[h] = carry
                    carry_max = carry if carry_max is None else jnp.maximum(carry_max, carry)
                    ws.append(w)
                acc_ref[p] += pair_pv(ws[0], ws[1], v_ref[pl.ds(c0, K_BLOCK), pair])
            return jj + 1, jnp.max(carry_max)

        lax.while_loop(more_tiles, k_tile, (jnp.int32(0), jnp.float32(0.0)))
        o_ref[pl.ds(r0, Q_BLOCK), :] = jnp.concatenate(
            [acc_ref[p] for p in range(len(pair_lanes))], axis=1).astype(o_ref.dtype)
        return 0

    lax.fori_loop(0, seq // Q_BLOCK, q_block, 0)


def _attention_skip_threshold(q_gain, k_gain):
    logit_bound = (SB_HEAD_DIM * SB_SCALE_LOG2 * LOGIT_BOUND_MARGIN
                   * jnp.max(jnp.abs(q_gain)) * jnp.max(jnp.abs(k_gain)))
    return (F32_EXP2_ZERO_BELOW - logit_bound).reshape(1).astype(F32)


def _attention(skip_below, q, k, v, tri, batch, seq):
    blk = pl.BlockSpec((seq, SB_WIDTH), lambda b: (b, 0))
    return pl.pallas_call(
        _attn_kernel,
        grid=(batch,),
        in_specs=[pl.BlockSpec(memory_space=pltpu.SMEM), blk, blk, blk,
                  _resident((K_BLOCK, K_BLOCK))],
        out_specs=blk,
        out_shape=jax.ShapeDtypeStruct((batch * seq, SB_WIDTH), BF16),
        scratch_shapes=[pltpu.VMEM((SB_HEADS // 2, Q_BLOCK, LANES), F32),
                        pltpu.VMEM((SB_HEADS, Q_BLOCK, LANES), F32)],
        compiler_params=_params("parallel"),
        name="sb_attention",
    )(skip_below, q, k, v, tri)


def _zero_after(v):
    bits = lax.bitcast_convert_type(v[0:1, 0:1], jnp.uint32)
    return ((bits >> 16) >> 16).astype(F32)


def _mixers_steps(a_ref, up_ref, up_halo_ref, uc_ref, uc_halo_ref, x_ref,
                  poolw_ref, pscale_ref, convw_ref, convb_ref, lng_ref, lnb_ref, pw_ref,
                  wout_ref, x1_ref, tile_in_seq, conv_done, ffn_done):
    tm = up_ref.shape[0]
    rows = tm + HALO
    has_prev = tile_in_seq > 0

    u = jnp.concatenate([jnp.where(has_prev, up_halo_ref[...], 0.0), up_ref[...]], axis=0)
    s2 = u + pltpu.roll(u, 1, 0)
    s4 = s2 + pltpu.roll(s2, 2, 0)
    s8 = s4 + pltpu.roll(s4, 4, 0)
    s16 = s8 + pltpu.roll(s8, 8, 0)
    group = lax.broadcasted_iota(jnp.int32, (tm, POOL_WIDTH), 1) // POOL_GROUP_DIM
    win_sum = jnp.where(group == 0, s2[HALO:],
                        jnp.where(group == 1, s4[HALO:],
                                  jnp.where(group == 2, s8[HALO:], s16[HALO:])))
    window = jnp.where(group == 0, float(POOL_WINDOWS[0]),
                       jnp.where(group == 1, float(POOL_WINDOWS[1]),
                                 jnp.where(group == 2, float(POOL_WINDOWS[2]),
                                           float(POOL_WINDOWS[3]))))
    pos = (tile_in_seq * tm + lax.broadcasted_iota(jnp.int32, (tm, POOL_WIDTH), 0)).astype(F32)
    pooled = win_sum / jnp.minimum(pos + 1.0, window) - u[HALO:]
    p_out = jnp.dot(pooled.astype(BF16), poolw_ref[...], preferred_element_type=F32)
    p_out = (p_out * pscale_ref[...]).astype(BF16)
    yield

    uc = jnp.concatenate([jnp.where(has_prev, uc_halo_ref[...], 0.0), uc_ref[...]], axis=0)
    glu = uc[:, :CONV_WIDTH] * jax.nn.sigmoid(uc[:, CONV_WIDTH:])
    shifted = [glu] + [pltpu.roll(glu, rows - r, 0) for r in range(1, SUBLANES)]
    yield
    first_tap = HALO - (CONV_KERNEL - 1)
    ys = []
    for r0 in range(0, tm, CONV_ROW_BLOCK):
        conv = None
        for kk in range(CONV_KERNEL):
            r, base = (first_tap + kk) % SUBLANES, (first_tap + kk) // SUBLANES * SUBLANES
            term = shifted[r][base + r0:base + r0 + CONV_ROW_BLOCK] * convw_ref[kk:kk + 1, :]
            conv = term if conv is None else conv + term
        conv = conv + (convb_ref[...] + ffn_done[(r0 // CONV_ROW_BLOCK) // CONV_PIECES_PER_CHUNK])
        mu = jnp.mean(conv, axis=-1, keepdims=True)
        cen = conv - mu
        var = jnp.mean(cen * cen, axis=-1, keepdims=True)
        y = cen * lax.rsqrt(var + LN_EPS) * lng_ref[...] + lnb_ref[...]
        y = y * jax.nn.sigmoid(y)
        conv_done.append(_zero_after(y))
        ys.append(y.astype(BF16))
        yield
    c_out = jnp.dot(jnp.concatenate(ys, axis=0), pw_ref[...], preferred_element_type=F32)

    mix = jnp.concatenate([a_ref[...], p_out, c_out.astype(BF16)], axis=-1)
    x1_ref[...] = x_ref[...] + jnp.dot(mix, wout_ref[...], preferred_element_type=F32)
    yield


def _ffn_steps(x1, g_ref, wgu_ref, wdown_ref, o_ref, conv_done, ffn_done):
    h = _rmsnorm(x1, g_ref[...]).astype(BF16)
    yield
    acc = x1
    for lo, hi in FFN_CHUNKS:
        hc = h if not conv_done else h + conv_done[-1].astype(BF16)
        gate = jnp.dot(hc, wgu_ref[:, lo:hi], preferred_element_type=F32)
        up = jnp.dot(hc, wgu_ref[:, FFN_HIDDEN + lo:FFN_HIDDEN + hi], preferred_element_type=F32)
        act = (gate * jax.nn.sigmoid(gate) * up).astype(BF16)
        acc = acc + jnp.dot(act, wdown_ref[lo:hi, :], preferred_element_type=F32)
        ffn_done.append(_zero_after(acc))
        yield
    o_ref[...] = acc
    yield


def _mix_ffn_kernel(a_ref, up_ref, up_halo_ref, uc_ref, uc_halo_ref, x_ref,
                    poolw_ref, pscale_ref, convw_ref, convb_ref, lng_ref, lnb_ref, pw_ref,
                    wout_ref, g_ref, wgu_ref, wdown_ref, o_ref, x1_ref, *, tiles_per_seq, n_tiles):
    i = pl.program_id(0)

    @pl.when(i == 0)
    def _():
        x1_ref[...] = jnp.zeros_like(x1_ref)

    tile = jnp.minimum(i, n_tiles - 1)
    conv_done, ffn_done = [], []
    ffn = _ffn_steps(x1_ref[...], g_ref, wgu_ref, wdown_ref, o_ref, conv_done, ffn_done)
    mixers = _mixers_steps(a_ref, up_ref, up_halo_ref, uc_ref, uc_halo_ref, x_ref,
                           poolw_ref, pscale_ref, convw_ref, convb_ref, lng_ref, lnb_ref,
                           pw_ref, wout_ref, x1_ref, tile % tiles_per_seq, conv_done, ffn_done)
    next(ffn)
    next(mixers)
    next(mixers)
    next(ffn)
    next(ffn)
    for s in range(len(FFN_CHUNKS) - 2):
        for _ in range(CONV_PIECES_PER_CHUNK):
            next(mixers)
        next(ffn)
    next(ffn)
    next(mixers)


def _mix_ffn(a, up, uc, x, poolw, pscale, convw, convb, lng, lnb, pw, wout, g, wgu, wdown, seq):
    n = x.shape[0]
    tm = TOKEN_TILE
    nt = n // tm
    cur = lambda i: jnp.minimum(i, nt - 1)
    row = lambda c: pl.BlockSpec((tm, c), lambda i: (cur(i), 0))
    halo = lambda c: pl.BlockSpec(
        (HALO, c), lambda i: (jnp.maximum(cur(i) * (tm // HALO) - 1, 0), 0))
    return pl.pallas_call(
        functools.partial(_mix_ffn_kernel, tiles_per_seq=seq // tm, n_tiles=nt),
        grid=(nt + 1,),
        in_specs=[row(SB_WIDTH), row(POOL_WIDTH), halo(POOL_WIDTH), row(2 * CONV_WIDTH),
                  halo(2 * CONV_WIDTH), row(D_MODEL),
                  _resident((POOL_WIDTH, POOL_WIDTH)), _resident((1, POOL_WIDTH)),
                  _resident((CONV_KERNEL, CONV_WIDTH)), _resident((1, CONV_WIDTH)),
                  _resident((1, CONV_WIDTH)), _resident((1, CONV_WIDTH)),
                  _resident((CONV_WIDTH, CONV_WIDTH)), _resident((D_MODEL, D_MODEL)),
                  _resident((1, D_MODEL)), _resident((D_MODEL, 2 * FFN_HIDDEN)),
                  _resident((FFN_HIDDEN, D_MODEL))],
        out_specs=pl.BlockSpec((tm, D_MODEL), lambda i: (jnp.maximum(i - 1, 0), 0)),
        out_shape=jax.ShapeDtypeStruct((n, D_MODEL), F32),
        scratch_shapes=[pltpu.VMEM((tm, D_MODEL), F32)],
        compiler_params=_params("arbitrary"),
        name="mixers_ffn",
    )(a, up, up, uc, uc, x, poolw, pscale, convw, convb, lng, lnb, pw, wout, g, wgu, wdown)


def kernel(x, norm_mix_g, w_in, sb_q_g, sb_k_g, pool_w, pool_scale, conv_w, conv_b, conv_ln_g,
           conv_ln_b, conv_pw, w_out, norm_ffn_g, ffn_w_gu, ffn_w_down):
    batch, seq, d = x.shape
    assert d == D_MODEL and seq % TOKEN_TILE == 0 and seq % Q_BLOCK == 0
    xf = x.reshape(batch * seq, d)

    head_of = jnp.arange(MXU_TILE) // SB_HEAD_DIM
    headavg = jnp.where(head_of[:, None] == head_of[None, :], 1.0 / SB_HEAD_DIM, 0.0).astype(BF16)
    kidx = jnp.arange(K_BLOCK)
    tri = -(kidx[:, None] >= kidx[None, :]).astype(BF16)
    row = lambda v: v.reshape(1, -1).astype(F32)

    for l in range(DEPTH):
        poolw_bd = jax.scipy.linalg.block_diag(*[pool_w[l, g] for g in range(len(POOL_WINDOWS))])
        q, k, v, up, uc = _inproj(
            xf, row(norm_mix_g[l]), w_in[l].astype(BF16),
            row(jnp.tile(sb_q_g[l], SB_HEADS)), row(jnp.tile(sb_k_g[l], SB_HEADS)), headavg)
        a = _attention(_attention_skip_threshold(sb_q_g[l], sb_k_g[l]), q, k, v, tri, batch, seq)
        xf = _mix_ffn(a, up, uc, xf, poolw_bd.astype(BF16), row(pool_scale[l]),
                      conv_w[l].astype(F32), row(conv_b[l]), row(conv_ln_g[l]),
                      row(conv_ln_b[l]), conv_pw[l].astype(BF16), w_out[l].astype(BF16),
                      row(norm_ffn_g[l]), ffn_w_gu[l].astype(BF16), ffn_w_down[l].astype(BF16),
                      seq)
    return xf.reshape(batch, seq, d)
```

```python
import functools
import math

import jax
import jax.numpy as jnp
from jax import lax
from jax.experimental import pallas as pl
from jax.experimental.pallas import tpu as pltpu

F32 = jnp.float32
BF16 = jnp.bfloat16

D_MODEL = 1024
DEPTH = 4
SB_HEADS = 8
SB_HEAD_DIM = 64
SB_WIDTH = SB_HEADS * SB_HEAD_DIM
POOL_WINDOWS = (2, 4, 8, 16)
POOL_GROUP_DIM = 64
POOL_WIDTH = len(POOL_WINDOWS) * POOL_GROUP_DIM
CONV_WIDTH = D_MODEL - SB_WIDTH - POOL_WIDTH
CONV_KERNEL = 31
IN_COLS = 3 * SB_WIDTH + POOL_WIDTH + 2 * CONV_WIDTH
FFN_HIDDEN = int(math.ceil((8 * D_MODEL / 3) / 256) * 256)
RMS_EPS = 1e-6
LN_EPS = 1e-5
LOG2_E = math.log2(math.e)
SB_SCALE_LOG2 = LOG2_E / math.sqrt(SB_HEAD_DIM)
F32_EXP2_ZERO_BELOW = -152.0
LOGIT_BOUND_MARGIN = 1.05

V7X_VMEM_LIMIT_BYTES = 56 * 1024 * 1024
LANES = 128
SUBLANES = 8
MXU_TILE = 256

TOKEN_TILE = 512
HALO = 32
K_BLOCK = MXU_TILE
Q_BLOCK = 2 * K_BLOCK
CONV_ROW_BLOCK = 128
FFN_CHUNKS = ((0, 768), (768, 1536), (1536, 2304), (2304, FFN_HIDDEN))

assert HALO >= CONV_KERNEL - 1 and HALO >= max(POOL_WINDOWS) - 1 and HALO % SUBLANES == 0
assert all(lo % MXU_TILE == 0 and hi % MXU_TILE == 0 for lo, hi in FFN_CHUNKS)
CONV_PIECES_PER_CHUNK = 2
assert TOKEN_TILE // CONV_ROW_BLOCK == CONV_PIECES_PER_CHUNK * (len(FFN_CHUNKS) - 2)
_NT = (((1,), (1,)), ((), ()))
_NN = (((1,), (0,)), ((), ()))


def _params(*semantics):
    return pltpu.CompilerParams(dimension_semantics=semantics,
                                vmem_limit_bytes=V7X_VMEM_LIMIT_BYTES)


def _resident(shape):
    return pl.BlockSpec(shape, lambda *_: (0,) * len(shape), pipeline_mode=pl.Buffered(1))


def _rmsnorm(x, g):
    ms = jnp.mean(x * x, axis=-1, keepdims=True)
    return x * lax.rsqrt(ms + RMS_EPS) * g


def _inproj_kernel(x_ref, g_ref, w_ref, qg_ref, kg_ref, headavg_ref,
                   q_ref, k_ref, v_ref, up_ref, uc_ref):
    h = _rmsnorm(x_ref[...], g_ref[...]).astype(BF16)

    def proj(lo, hi):
        return jnp.dot(h, w_ref[:, lo:hi], preferred_element_type=F32)

    def head_norm(t, g):
        sq = (t * t).astype(BF16)
        ms = jnp.concatenate(
            [jnp.dot(sq[:, c:c + MXU_TILE], headavg_ref[...], preferred_element_type=F32)
             for c in range(0, SB_WIDTH, MXU_TILE)], axis=-1)
        return t * lax.rsqrt(ms + RMS_EPS) * g

    q_ref[...] = (head_norm(proj(0, SB_WIDTH), qg_ref[...]) * SB_SCALE_LOG2).astype(BF16)
    k_ref[...] = head_norm(proj(SB_WIDTH, 2 * SB_WIDTH), kg_ref[...]).astype(BF16)
    v_ref[...] = proj(2 * SB_WIDTH, 3 * SB_WIDTH).astype(BF16)
    up_ref[...] = proj(3 * SB_WIDTH, 3 * SB_WIDTH + POOL_WIDTH)
    uc_ref[...] = proj(3 * SB_WIDTH + POOL_WIDTH, IN_COLS)


def _inproj(x, g, w, qg, kg, headavg):
    n = x.shape[0]
    tm = TOKEN_TILE
    row = lambda c: pl.BlockSpec((tm, c), lambda i: (i, 0))
    return pl.pallas_call(
        _inproj_kernel,
        grid=(n // tm,),
        in_specs=[row(D_MODEL), _resident((1, D_MODEL)), _resident((D_MODEL, IN_COLS)),
                  _resident((1, SB_WIDTH)), _resident((1, SB_WIDTH)),
                  _resident((MXU_TILE, MXU_TILE))],
        out_specs=[row(SB_WIDTH), row(SB_WIDTH), row(SB_WIDTH), row(POOL_WIDTH),
                   row(2 * CONV_WIDTH)],
        out_shape=[jax.ShapeDtypeStruct((n, SB_WIDTH), BF16)] * 3
        + [jax.ShapeDtypeStruct((n, POOL_WIDTH), F32),
           jax.ShapeDtypeStruct((n, 2 * CONV_WIDTH), F32)],
        compiler_params=_params("parallel"),
        name="inproj",
    )(x, g, w, qg, kg, headavg)


def _attn_kernel(skip_below_ref, q_ref, k_ref, v_ref, tri_ref, o_ref, acc_ref, carry_ref):
    seq = q_ref.shape[0]
    half = Q_BLOCK // 2
    head_lanes = [slice(h * SB_HEAD_DIM, (h + 1) * SB_HEAD_DIM) for h in range(SB_HEADS)]
    pair_lanes = [slice(p * LANES, (p + 1) * LANES) for p in range(SB_HEADS // 2)]
    lane_is_first = lax.broadcasted_iota(jnp.int32, (K_BLOCK, LANES), 1) < SB_HEAD_DIM

    def strict(rows):
        return (lax.broadcasted_iota(jnp.int32, (rows, K_BLOCK), 1)
                < lax.broadcasted_iota(jnp.int32, (rows, K_BLOCK), 0))

    def lane_bcast(col):
        return jnp.broadcast_to(col, (col.shape[0], LANES))

    def tile(qh, kt, carry, mask):
        z = lax.dot_general(qh, kt, _NT, preferred_element_type=F32)
        neg_abs = lax.bitcast_convert_type(
            lax.bitcast_convert_type(z, jnp.uint32) | jnp.uint32(0x80000000), F32)
        softplus = jnp.maximum(z, 0.0) + jnp.log(1.0 + jnp.exp2(neg_abs)) * LOG2_E
        if mask is not None:
            softplus = jnp.where(mask, softplus, 0.0)
        incl = lax.dot_general(softplus, tri_ref[...], _NN, preferred_element_type=F32)
        expo = z + incl
        if carry is not None:
            expo = expo + jnp.concatenate([carry] * (K_BLOCK // LANES), axis=1)
        w = jnp.exp2(expo)
        if mask is not None:
            w = jnp.where(mask, w, 0.0)
        if mask is None:
            new_carry = lane_bcast(expo[:, 0:1] - z[:, 0:1])
        else:
            new_carry = lane_bcast(incl[:, 0:1])
            if carry is not None:
                new_carry = new_carry + carry
        return w, new_carry

    def pair_pv(w_first, w_second, v_pair):
        zero = jnp.zeros_like(v_pair)
        v_both = jnp.concatenate([jnp.where(lane_is_first, v_pair, zero),
                                  jnp.where(lane_is_first, zero, v_pair)], axis=0)
        return lax.dot_general(jnp.concatenate([w_first, w_second], axis=1), v_both, _NN,
                               preferred_element_type=F32)

    def q_block(qi, _):
        r0 = pl.multiple_of(qi * Q_BLOCK, Q_BLOCK)
        r1 = pl.multiple_of(qi * Q_BLOCK + half, half)
        halves = (slice(0, half), slice(half, Q_BLOCK))
        carry_max = [None, None]
        for p, pair in enumerate(pair_lanes):
            ws_b, ws = [], []
            for h in (2 * p, 2 * p + 1):
                lanes = head_lanes[h]
                w_b, carry_b = tile(q_ref[pl.ds(r1, half), lanes],
                                    k_ref[pl.ds(r1, K_BLOCK), lanes], None, strict(half))
                carry0 = jnp.concatenate([jnp.zeros((half, LANES), F32), carry_b], axis=0)
                w, carry = tile(q_ref[pl.ds(r0, Q_BLOCK), lanes],
                                k_ref[pl.ds(r0, K_BLOCK), lanes], carry0, strict(Q_BLOCK))
                carry_ref[h] = carry
                carry_max = [carry[rows] if m is None else jnp.maximum(m, carry[rows])
                             for m, rows in zip(carry_max, halves)]
                ws_b.append(w_b)
                ws.append(w)
            pv_b = pair_pv(ws_b[0], ws_b[1], v_ref[pl.ds(r1, K_BLOCK), pair])
            pv = pair_pv(ws[0], ws[1], v_ref[pl.ds(r0, K_BLOCK), pair])
            acc_ref[p] = pv + jnp.concatenate([jnp.zeros((half, LANES), F32), pv_b], axis=0)

        def more_tiles(state):
            jj, worst_carry = state
            return jnp.logical_and(jj < 2 * qi, worst_carry > skip_below_ref[0])

        for rows, worst in zip(halves, carry_max):
            def k_tile(state, rows=rows):
                jj, _ = state
                c0 = pl.multiple_of((2 * qi - 1 - jj) * K_BLOCK, K_BLOCK)
                worst = None
                for p, pair in enumerate(pair_lanes):
                    ws = []
                    for h in (2 * p, 2 * p + 1):
                        w, carry = tile(q_ref[pl.ds(r0 + rows.start, half), head_lanes[h]],
                                        k_ref[pl.ds(c0, K_BLOCK), head_lanes[h]],
                                        carry_ref[h, rows], None)
                        carry_ref[h, rows] = carry
                        worst = carry if worst is None else jnp.maximum(worst, carry)
                        ws.append(w)
                    acc_ref[p, rows] += pair_pv(ws[0], ws[1], v_ref[pl.ds(c0, K_BLOCK), pair])
                return jj + 1, jnp.max(worst)

            lax.while_loop(more_tiles, k_tile, (jnp.int32(0), jnp.max(worst)))

        o_ref[pl.ds(r0, Q_BLOCK), :] = jnp.concatenate(
            [acc_ref[p] for p in range(len(pair_lanes))], axis=1).astype(o_ref.dtype)
        return 0

    lax.fori_loop(0, seq // Q_BLOCK, q_block, 0)


def _attention_skip_threshold(q_gain, k_gain):
    logit_bound = (SB_HEAD_DIM * SB_SCALE_LOG2 * LOGIT_BOUND_MARGIN
                   * jnp.max(jnp.abs(q_gain)) * jnp.max(jnp.abs(k_gain)))
    return (F32_EXP2_ZERO_BELOW - logit_bound).reshape(1).astype(F32)


def _attention(skip_below, q, k, v, tri, batch, seq):
    blk = pl.BlockSpec((seq, SB_WIDTH), lambda b: (b, 0))
    return pl.pallas_call(
        _attn_kernel,
        grid=(batch,),
        in_specs=[pl.BlockSpec(memory_space=pltpu.SMEM), blk, blk, blk,
                  _resident((K_BLOCK, K_BLOCK))],
        out_specs=blk,
        out_shape=jax.ShapeDtypeStruct((batch * seq, SB_WIDTH), BF16),
        scratch_shapes=[pltpu.VMEM((SB_HEADS // 2, Q_BLOCK, LANES), F32),
                        pltpu.VMEM((SB_HEADS, Q_BLOCK, LANES), F32)],
        compiler_params=_params("parallel"),
        name="sb_attention",
    )(skip_below, q, k, v, tri)


def _zero_after(v):
    bits = lax.bitcast_convert_type(v[0:1, 0:1], jnp.uint32)
    return ((bits >> 16) >> 16).astype(F32)


def _mixers_steps(a_ref, up_ref, up_halo_ref, uc_ref, uc_halo_ref, x_ref,
                  poolw_ref, pscale_ref, convw_ref, convb_ref, lng_ref, lnb_ref, pw_ref,
                  wout_ref, x1_ref, tile_in_seq, conv_done, ffn_done):
    tm = up_ref.shape[0]
    rows = tm + HALO
    has_prev = tile_in_seq > 0

    u = jnp.concatenate([jnp.where(has_prev, up_halo_ref[...], 0.0), up_ref[...]], axis=0)
    s2 = u + pltpu.roll(u, 1, 0)
    s4 = s2 + pltpu.roll(s2, 2, 0)
    s8 = s4 + pltpu.roll(s4, 4, 0)
    s16 = s8 + pltpu.roll(s8, 8, 0)
    group = lax.broadcasted_iota(jnp.int32, (tm, POOL_WIDTH), 1) // POOL_GROUP_DIM
    win_sum = jnp.where(group == 0, s2[HALO:],
                        jnp.where(group == 1, s4[HALO:],
                                  jnp.where(group == 2, s8[HALO:], s16[HALO:])))
    window = jnp.where(group == 0, float(POOL_WINDOWS[0]),
                       jnp.where(group == 1, float(POOL_WINDOWS[1]),
                                 jnp.where(group == 2, float(POOL_WINDOWS[2]),
                                           float(POOL_WINDOWS[3]))))
    pos = (tile_in_seq * tm + lax.broadcasted_iota(jnp.int32, (tm, POOL_WIDTH), 0)).astype(F32)
    pooled = win_sum / jnp.minimum(pos + 1.0, window) - u[HALO:]
    p_out = jnp.dot(pooled.astype(BF16), poolw_ref[...], preferred_element_type=F32)
    p_out = (p_out * pscale_ref[...]).astype(BF16)
    yield

    uc = jnp.concatenate([jnp.where(has_prev, uc_halo_ref[...], 0.0), uc_ref[...]], axis=0)
    glu = uc[:, :CONV_WIDTH] * jax.nn.sigmoid(uc[:, CONV_WIDTH:])
    shifted = [glu] + [pltpu.roll(glu, rows - r, 0) for r in range(1, SUBLANES)]
    yield
    first_tap = HALO - (CONV_KERNEL - 1)
    ys = []
    for r0 in range(0, tm, CONV_ROW_BLOCK):
        conv = None
        for kk in range(CONV_KERNEL):
            r, base = (first_tap + kk) % SUBLANES, (first_tap + kk) // SUBLANES * SUBLANES
            term = shifted[r][base + r0:base + r0 + CONV_ROW_BLOCK] * convw_ref[kk:kk + 1, :]
            conv = term if conv is None else conv + term
        conv = conv + (convb_ref[...] + ffn_done[(r0 // CONV_ROW_BLOCK) // CONV_PIECES_PER_CHUNK])
        mu = jnp.mean(conv, axis=-1, keepdims=True)
        cen = conv - mu
        var = jnp.mean(cen * cen, axis=-1, keepdims=True)
        y = cen * lax.rsqrt(var + LN_EPS) * lng_ref[...] + lnb_ref[...]
        y = y * jax.nn.sigmoid(y)
        conv_done.append(_zero_after(y))
        ys.append(y.astype(BF16))
        yield
    c_out = jnp.dot(jnp.concatenate(ys, axis=0), pw_ref[...], preferred_element_type=F32)

    mix = jnp.concatenate([a_ref[...], p_out, c_out.astype(BF16)], axis=-1)
    x1_ref[...] = x_ref[...] + jnp.dot(mix, wout_ref[...], preferred_element_type=F32)
    yield


def _ffn_steps(x1, g_ref, wgu_ref, wdown_ref, o_ref, conv_done, ffn_done):
    h = _rmsnorm(x1, g_ref[...]).astype(BF16)
    yield
    acc = x1
    for lo, hi in FFN_CHUNKS:
        hc = h if not conv_done else h + conv_done[-1].astype(BF16)
        gate = jnp.dot(hc, wgu_ref[:, lo:hi], preferred_element_type=F32)
        up = jnp.dot(hc, wgu_ref[:, FFN_HIDDEN + lo:FFN_HIDDEN + hi], preferred_element_type=F32)
        act = (gate * jax.nn.sigmoid(gate) * up).astype(BF16)
        acc = acc + jnp.dot(act, wdown_ref[lo:hi, :], preferred_element_type=F32)
        ffn_done.append(_zero_after(acc))
        yield
    o_ref[...] = acc
    yield


def _mix_ffn_kernel(a_ref, up_ref, up_halo_ref, uc_ref, uc_halo_ref, x_ref,
                    poolw_ref, pscale_ref, convw_ref, convb_ref, lng_ref, lnb_ref, pw_ref,
                    wout_ref, g_ref, wgu_ref, wdown_ref, o_ref, x1_ref, *, tiles_per_seq, n_tiles):
    i = pl.program_id(0)

    @pl.when(i == 0)
    def _():
        x1_ref[...] = jnp.zeros_like(x1_ref)

    tile = jnp.minimum(i, n_tiles - 1)
    conv_done, ffn_done = [], []
    ffn = _ffn_steps(x1_ref[...], g_ref, wgu_ref, wdown_ref, o_ref, conv_done, ffn_done)
    mixers = _mixers_steps(a_ref, up_ref, up_halo_ref, uc_ref, uc_halo_ref, x_ref,
                           poolw_ref, pscale_ref, convw_ref, convb_ref, lng_ref, lnb_ref,
                           pw_ref, wout_ref, x1_ref, tile % tiles_per_seq, conv_done, ffn_done)
    next(ffn)
    next(mixers)
    next(mixers)
    next(ffn)
    next(ffn)
    for s in range(len(FFN_CHUNKS) - 2):
        for _ in range(CONV_PIECES_PER_CHUNK):
            next(mixers)
        next(ffn)
    next(ffn)
    next(mixers)


def _mix_ffn(a, up, uc, x, poolw, pscale, convw, convb, lng, lnb, pw, wout, g, wgu, wdown, seq):
    n = x.shape[0]
    tm = TOKEN_TILE
    nt = n // tm
    cur = lambda i: jnp.minimum(i, nt - 1)
    row = lambda c: pl.BlockSpec((tm, c), lambda i: (cur(i), 0))
    halo = lambda c: pl.BlockSpec(
        (HALO, c), lambda i: (jnp.maximum(cur(i) * (tm // HALO) - 1, 0), 0))
    return pl.pallas_call(
        functools.partial(_mix_ffn_kernel, tiles_per_seq=seq // tm, n_tiles=nt),
        grid=(nt + 1,),
        in_specs=[row(SB_WIDTH), row(POOL_WIDTH), halo(POOL_WIDTH), row(2 * CONV_WIDTH),
                  halo(2 * CONV_WIDTH), row(D_MODEL),
                  _resident((POOL_WIDTH, POOL_WIDTH)), _resident((1, POOL_WIDTH)),
                  _resident((CONV_KERNEL, CONV_WIDTH)), _resident((1, CONV_WIDTH)),
                  _resident((1, CONV_WIDTH)), _resident((1, CONV_WIDTH)),
                  _resident((CONV_WIDTH, CONV_WIDTH)), _resident((D_MODEL, D_MODEL)),
                  _resident((1, D_MODEL)), _resident((D_MODEL, 2 * FFN_HIDDEN)),
                  _resident((FFN_HIDDEN, D_MODEL))],
        out_specs=pl.BlockSpec((tm, D_MODEL), lambda i: (jnp.maximum(i - 1, 0), 0)),
        out_shape=jax.ShapeDtypeStruct((n, D_MODEL), F32),
        scratch_shapes=[pltpu.VMEM((tm, D_MODEL), F32)],
        compiler_params=_params("arbitrary"),
        name="mixers_ffn",
    )(a, up, up, uc, uc, x, poolw, pscale, convw, convb, lng, lnb, pw, wout, g, wgu, wdown)


def kernel(x, norm_mix_g, w_in, sb_q_g, sb_k_g, pool_w, pool_scale, conv_w, conv_b, conv_ln_g,
           conv_ln_b, conv_pw, w_out, norm_ffn_g, ffn_w_gu, ffn_w_down):
    batch, seq, d = x.shape
    assert d == D_MODEL and seq % TOKEN_TILE == 0 and seq % Q_BLOCK == 0
    xf = x.reshape(batch * seq, d)

    head_of = jnp.arange(MXU_TILE) // SB_HEAD_DIM
    headavg = jnp.where(head_of[:, None] == head_of[None, :], 1.0 / SB_HEAD_DIM, 0.0).astype(BF16)
    kidx = jnp.arange(K_BLOCK)
    tri = -(kidx[:, None] >= kidx[None, :]).astype(BF16)
    row = lambda v: v.reshape(1, -1).astype(F32)

    for l in range(DEPTH):
        poolw_bd = jax.scipy.linalg.block_diag(*[pool_w[l, g] for g in range(len(POOL_WINDOWS))])
        q, k, v, up, uc = _inproj(
            xf, row(norm_mix_g[l]), w_in[l].astype(BF16),
            row(jnp.tile(sb_q_g[l], SB_HEADS)), row(jnp.tile(sb_k_g[l], SB_HEADS)), headavg)
        a = _attention(_attention_skip_threshold(sb_q_g[l], sb_k_g[l]), q, k, v, tri, batch, seq)
        xf = _mix_ffn(a, up, uc, xf, poolw_bd.astype(BF16), row(pool_scale[l]),
                      conv_w[l].astype(F32), row(conv_b[l]), row(conv_ln_g[l]),
                      row(conv_ln_b[l]), conv_pw[l].astype(BF16), w_out[l].astype(BF16),
                      row(norm_ffn_g[l]), ffn_w_gu[l].astype(BF16), ffn_w_down[l].astype(BF16),
                      seq)
    return xf.reshape(batch, seq, d)
```

```python
import functools
import math

import jax
import jax.numpy as jnp
from jax import lax
from jax.experimental import pallas as pl
from jax.experimental.pallas import tpu as pltpu

F32 = jnp.float32
BF16 = jnp.bfloat16

D_MODEL = 1024
DEPTH = 4
SB_HEADS = 8
SB_HEAD_DIM = 64
SB_WIDTH = SB_HEADS * SB_HEAD_DIM
POOL_WINDOWS = (2, 4, 8, 16)
POOL_GROUP_DIM = 64
POOL_WIDTH = len(POOL_WINDOWS) * POOL_GROUP_DIM
CONV_WIDTH = D_MODEL - SB_WIDTH - POOL_WIDTH
CONV_KERNEL = 31
IN_COLS = 3 * SB_WIDTH + POOL_WIDTH + 2 * CONV_WIDTH
FFN_HIDDEN = int(math.ceil((8 * D_MODEL / 3) / 256) * 256)
RMS_EPS = 1e-6
LN_EPS = 1e-5
LOG2_E = math.log2(math.e)
SB_SCALE_LOG2 = LOG2_E / math.sqrt(SB_HEAD_DIM)
F32_EXP2_ZERO_BELOW = -152.0
LOGIT_BOUND_MARGIN = 1.05

V7X_VMEM_LIMIT_BYTES = 56 * 1024 * 1024
LANES = 128
SUBLANES = 8
MXU_TILE = 256

TOKEN_TILE = 512
HALO = 32
K_BLOCK = MXU_TILE
Q_BLOCK = 2 * K_BLOCK
CONV_ROW_BLOCK = 128
FFN_CHUNKS = ((0, 768), (768, 1536), (1536, 2304), (2304, FFN_HIDDEN))

assert HALO >= CONV_KERNEL - 1 and HALO >= max(POOL_WINDOWS) - 1 and HALO % SUBLANES == 0
assert all(lo % MXU_TILE == 0 and hi % MXU_TILE == 0 for lo, hi in FFN_CHUNKS)
CONV_PIECES_PER_CHUNK = 2
assert TOKEN_TILE // CONV_ROW_BLOCK == CONV_PIECES_PER_CHUNK * (len(FFN_CHUNKS) - 2)
_NT = (((1,), (1,)), ((), ()))
_NN = (((1,), (0,)), ((), ()))


def _params(*semantics):
    return pltpu.CompilerParams(dimension_semantics=semantics,
                                vmem_limit_bytes=V7X_VMEM_LIMIT_BYTES)


def _resident(shape):
    return pl.BlockSpec(shape, lambda *_: (0,) * len(shape), pipeline_mode=pl.Buffered(1))


def _rmsnorm(x, g):
    ms = jnp.mean(x * x, axis=-1, keepdims=True)
    return x * lax.rsqrt(ms + RMS_EPS) * g


def _inproj_kernel(x_ref, g_ref, w_ref, qg_ref, kg_ref, headavg_ref,
                   q_ref, k_ref, v_ref, up_ref, uc_ref):
    h = _rmsnorm(x_ref[...], g_ref[...]).astype(BF16)

    def proj(lo, hi):
        return jnp.dot(h, w_ref[:, lo:hi], preferred_element_type=F32)

    def head_norm(t, g):
        sq = (t * t).astype(BF16)
        ms = jnp.concatenate(
            [jnp.dot(sq[:, c:c + MXU_TILE], headavg_ref[...], preferred_element_type=F32)
             for c in range(0, SB_WIDTH, MXU_TILE)], axis=-1)
        return t * lax.rsqrt(ms + RMS_EPS) * g

    q_ref[...] = (head_norm(proj(0, SB_WIDTH), qg_ref[...]) * SB_SCALE_LOG2).astype(BF16)
    k_ref[...] = head_norm(proj(SB_WIDTH, 2 * SB_WIDTH), kg_ref[...]).astype(BF16)
    v_ref[...] = proj(2 * SB_WIDTH, 3 * SB_WIDTH).astype(BF16)
    up_ref[...] = proj(3 * SB_WIDTH, 3 * SB_WIDTH + POOL_WIDTH)
    uc_ref[...] = proj(3 * SB_WIDTH + POOL_WIDTH, IN_COLS)


def _inproj(x, g, w, qg, kg, headavg):
    n = x.shape[0]
    tm = TOKEN_TILE
    row = lambda c: pl.BlockSpec((tm, c), lambda i: (i, 0))
    return pl.pallas_call(
        _inproj_kernel,
        grid=(n // tm,),
        in_specs=[row(D_MODEL), _resident((1, D_MODEL)), _resident((D_MODEL, IN_COLS)),
                  _resident((1, SB_WIDTH)), _resident((1, SB_WIDTH)),
                  _resident((MXU_TILE, MXU_TILE))],
        out_specs=[row(SB_WIDTH), row(SB_WIDTH), row(SB_WIDTH), row(POOL_WIDTH),
                   row(2 * CONV_WIDTH)],
        out_shape=[jax.ShapeDtypeStruct((n, SB_WIDTH), BF16)] * 3
        + [jax.ShapeDtypeStruct((n, POOL_WIDTH), F32),
           jax.ShapeDtypeStruct((n, 2 * CONV_WIDTH), F32)],
        compiler_params=_params("parallel"),
        name="inproj",
    )(x, g, w, qg, kg, headavg)


def _attn_kernel(skip_below_ref, q_ref, k_ref, v_ref, tri_ref, o_ref, acc_ref, carry_ref):
    seq = q_ref.shape[0]
    half = Q_BLOCK // 2
    head_lanes = [slice(h * SB_HEAD_DIM, (h + 1) * SB_HEAD_DIM) for h in range(SB_HEADS)]
    pair_lanes = [slice(p * LANES, (p + 1) * LANES) for p in range(SB_HEADS // 2)]
    lane_is_first = lax.broadcasted_iota(jnp.int32, (K_BLOCK, LANES), 1) < SB_HEAD_DIM

    def strict(rows):
        return (lax.broadcasted_iota(jnp.int32, (rows, K_BLOCK), 1)
                < lax.broadcasted_iota(jnp.int32, (rows, K_BLOCK), 0))

    def lane_bcast(col):
        return jnp.broadcast_to(col, (col.shape[0], LANES))

    def tile(qh, kt, carry, mask):
        z = lax.dot_general(qh, kt, _NT, preferred_element_type=F32)
        neg_abs = lax.bitcast_convert_type(
            lax.bitcast_convert_type(z, jnp.uint32) | jnp.uint32(0x80000000), F32)
        softplus = jnp.maximum(z, 0.0) + jnp.log(1.0 + jnp.exp2(neg_abs)) * LOG2_E
        if mask is not None:
            softplus = jnp.where(mask, softplus, 0.0)
        incl = lax.dot_general(softplus, tri_ref[...], _NN, preferred_element_type=F32)
        expo = z + incl
        if carry is not None:
            expo = expo + jnp.concatenate([carry] * (K_BLOCK // LANES), axis=1)
        w = jnp.exp2(jnp.minimum(expo, 0.0))
        if mask is not None:
            w = jnp.where(mask, w, 0.0)
        if mask is None:
            new_carry = lane_bcast(expo[:, 0:1] - z[:, 0:1])
        else:
            new_carry = lane_bcast(incl[:, 0:1])
            if carry is not None:
                new_carry = new_carry + carry
        return w, new_carry

    def pair_pv(w_first, w_second, v_pair):
        zero = jnp.zeros_like(v_pair)
        v_both = jnp.concatenate([jnp.where(lane_is_first, v_pair, zero),
                                  jnp.where(lane_is_first, zero, v_pair)], axis=0)
        return lax.dot_general(jnp.concatenate([w_first, w_second], axis=1), v_both, _NN,
                               preferred_element_type=F32)

    def q_block(qi, _):
        r0 = pl.multiple_of(qi * Q_BLOCK, Q_BLOCK)
        r1 = pl.multiple_of(qi * Q_BLOCK + half, half)
        halves = (slice(0, half), slice(half, Q_BLOCK))
        carry_max = [None, None]
        for p, pair in enumerate(pair_lanes):
            ws_b, ws = [], []
            for h in (2 * p, 2 * p + 1):
                lanes = head_lanes[h]
                w_b, carry_b = tile(q_ref[pl.ds(r1, half), lanes],
                                    k_ref[pl.ds(r1, K_BLOCK), lanes], None, strict(half))
                carry0 = jnp.concatenate([jnp.zeros((half, LANES), F32), carry_b], axis=0)
                w, carry = tile(q_ref[pl.ds(r0, Q_BLOCK), lanes],
                                k_ref[pl.ds(r0, K_BLOCK), lanes], carry0, strict(Q_BLOCK))
                carry_ref[h] = carry
                carry_max = [carry[rows] if m is None else jnp.maximum(m, carry[rows])
                             for m, rows in zip(carry_max, halves)]
                ws_b.append(w_b)
                ws.append(w)
            pv_b = pair_pv(ws_b[0], ws_b[1], v_ref[pl.ds(r1, K_BLOCK), pair])
            pv = pair_pv(ws[0], ws[1], v_ref[pl.ds(r0, K_BLOCK), pair])
            acc_ref[p] = pv + jnp.concatenate([jnp.zeros((half, LANES), F32), pv_b], axis=0)

        def more_tiles(state):
            jj, worst_carry = state
            return jnp.logical_and(jj < 2 * qi, worst_carry > skip_below_ref[0])

        for rows, worst in zip(halves, carry_max):
            def k_tile(state, rows=rows):
                jj, _ = state
                c0 = pl.multiple_of((2 * qi - 1 - jj) * K_BLOCK, K_BLOCK)
                worst = None
                for p, pair in enumerate(pair_lanes):
                    ws = []
                    for h in (2 * p, 2 * p + 1):
                        w, carry = tile(q_ref[pl.ds(r0 + rows.start, half), head_lanes[h]],
                                        k_ref[pl.ds(c0, K_BLOCK), head_lanes[h]],
                                        carry_ref[h, rows], None)
                        carry_ref[h, rows] = carry
                        worst = carry if worst is None else jnp.maximum(worst, carry)
                        ws.append(w)
                    acc_ref[p, rows] += pair_pv(ws[0], ws[1], v_ref[pl.ds(c0, K_BLOCK), pair])
                return jj + 1, jnp.max(worst)

            lax.while_loop(more_tiles, k_tile, (jnp.int32(0), jnp.max(worst)))

        o_ref[pl.ds(r0, Q_BLOCK), :] = jnp.concatenate(
            [acc_ref[p] for p in range(len(pair_lanes))], axis=1).astype(o_ref.dtype)
        return 0

    lax.fori_loop(0, seq // Q_BLOCK, q_block, 0)


def _attention_skip_threshold(q_gain, k_gain):
    logit_bound = (SB_HEAD_DIM * SB_SCALE_LOG2 * LOGIT_BOUND_MARGIN
                   * jnp.max(jnp.abs(q_gain)) * jnp.max(jnp.abs(k_gain)))
    return (F32_EXP2_ZERO_BELOW - logit_bound).reshape(1).astype(F32)


def _attention(skip_below, q, k, v, tri, batch, seq):
    blk = pl.BlockSpec((seq, SB_WIDTH), lambda b: (b, 0))
    return pl.pallas_call(
        _attn_kernel,
        grid=(batch,),
        in_specs=[pl.BlockSpec(memory_space=pltpu.SMEM), blk, blk, blk,
                  _resident((K_BLOCK, K_BLOCK))],
        out_specs=blk,
        out_shape=jax.ShapeDtypeStruct((batch * seq, SB_WIDTH), BF16),
        scratch_shapes=[pltpu.VMEM((SB_HEADS // 2, Q_BLOCK, LANES), F32),
                        pltpu.VMEM((SB_HEADS, Q_BLOCK, LANES), F32)],
        compiler_params=_params("parallel"),
        name="sb_attention",
    )(skip_below, q, k, v, tri)


def _zero_after(v):
    bits = lax.bitcast_convert_type(v[0:1, 0:1], jnp.uint32)
    return ((bits >> 16) >> 16).astype(F32)


def _mixers_steps(a_ref, up_ref, up_halo_ref, uc_ref, uc_halo_ref, x_ref,
                  poolw_ref, pscale_ref, convw_ref, convb_ref, lng_ref, lnb_ref, pw_ref,
                  wout_ref, x1_ref, tile_in_seq, conv_done, ffn_done):
    tm = up_ref.shape[0]
    rows = tm + HALO
    has_prev = tile_in_seq > 0

    u = jnp.concatenate([jnp.where(has_prev, up_halo_ref[...], 0.0), up_ref[...]], axis=0)
    s2 = u + pltpu.roll(u, 1, 0)
    s4 = s2 + pltpu.roll(s2, 2, 0)
    s8 = s4 + pltpu.roll(s4, 4, 0)
    s16 = s8 + pltpu.roll(s8, 8, 0)
    group = lax.broadcasted_iota(jnp.int32, (tm, POOL_WIDTH), 1) // POOL_GROUP_DIM
    win_sum = jnp.where(group == 0, s2[HALO:],
                        jnp.where(group == 1, s4[HALO:],
                                  jnp.where(group == 2, s8[HALO:], s16[HALO:])))
    window = jnp.where(group == 0, float(POOL_WINDOWS[0]),
                       jnp.where(group == 1, float(POOL_WINDOWS[1]),
                                 jnp.where(group == 2, float(POOL_WINDOWS[2]),
                                           float(POOL_WINDOWS[3]))))
    pos = (tile_in_seq * tm + lax.broadcasted_iota(jnp.int32, (tm, POOL_WIDTH), 0)).astype(F32)
    pooled = win_sum / jnp.minimum(pos + 1.0, window) - u[HALO:]
    p_out = jnp.dot(pooled.astype(BF16), poolw_ref[...], preferred_element_type=F32)
    p_out = (p_out * pscale_ref[...]).astype(BF16)
    yield

    uc = jnp.concatenate([jnp.where(has_prev, uc_halo_ref[...], 0.0), uc_ref[...]], axis=0)
    glu = uc[:, :CONV_WIDTH] * jax.nn.sigmoid(uc[:, CONV_WIDTH:])
    shifted = [glu] + [pltpu.roll(glu, rows - r, 0) for r in range(1, SUBLANES)]
    yield
    first_tap = HALO - (CONV_KERNEL - 1)
    ys = []
    for r0 in range(0, tm, CONV_ROW_BLOCK):
        conv = None
        for kk in range(CONV_KERNEL):
            r, base = (first_tap + kk) % SUBLANES, (first_tap + kk) // SUBLANES * SUBLANES
            term = shifted[r][base + r0:base + r0 + CONV_ROW_BLOCK] * convw_ref[kk:kk + 1, :]
            conv = term if conv is None else conv + term
        conv = conv + (convb_ref[...] + ffn_done[(r0 // CONV_ROW_BLOCK) // CONV_PIECES_PER_CHUNK])
        mu = jnp.mean(conv, axis=-1, keepdims=True)
        cen = conv - mu
        var = jnp.mean(cen * cen, axis=-1, keepdims=True)
        y = cen * lax.rsqrt(var + LN_EPS) * lng_ref[...] + lnb_ref[...]
        y = y * jax.nn.sigmoid(y)
        conv_done.append(_zero_after(y))
        ys.append(y.astype(BF16))
        yield
    c_out = jnp.dot(jnp.concatenate(ys, axis=0), pw_ref[...], preferred_element_type=F32)

    mix = jnp.concatenate([a_ref[...], p_out, c_out.astype(BF16)], axis=-1)
    x1_ref[...] = x_ref[...] + jnp.dot(mix, wout_ref[...], preferred_element_type=F32)
    yield


def _ffn_steps(x1, g_ref, wgu_ref, wdown_ref, o_ref, conv_done, ffn_done):
    h = _rmsnorm(x1, g_ref[...]).astype(BF16)
    yield
    acc = x1
    for lo, hi in FFN_CHUNKS:
        hc = h if not conv_done else h + conv_done[-1].astype(BF16)
        gate = jnp.dot(hc, wgu_ref[:, lo:hi], preferred_element_type=F32)
        up = jnp.dot(hc, wgu_ref[:, FFN_HIDDEN + lo:FFN_HIDDEN + hi], preferred_element_type=F32)
        act = (gate * jax.nn.sigmoid(gate) * up).astype(BF16)
        acc = acc + jnp.dot(act, wdown_ref[lo:hi, :], preferred_element_type=F32)
        ffn_done.append(_zero_after(acc))
        yield
    o_ref[...] = acc
    yield


def _mix_ffn_kernel(a_ref, up_ref, up_halo_ref, uc_ref, uc_halo_ref, x_ref,
                    poolw_ref, pscale_ref, convw_ref, convb_ref, lng_ref, lnb_ref, pw_ref,
                    wout_ref, g_ref, wgu_ref, wdown_ref, o_ref, x1_ref, *, tiles_per_seq, n_tiles):
    i = pl.program_id(0)

    @pl.when(i == 0)
    def _():
        x1_ref[...] = jnp.zeros_like(x1_ref)

    tile = jnp.minimum(i, n_tiles - 1)
    conv_done, ffn_done = [], []
    ffn = _ffn_steps(x1_ref[...], g_ref, wgu_ref, wdown_ref, o_ref, conv_done, ffn_done)
    mixers = _mixers_steps(a_ref, up_ref, up_halo_ref, uc_ref, uc_halo_ref, x_ref,
                           poolw_ref, pscale_ref, convw_ref, convb_ref, lng_ref, lnb_ref,
                           pw_ref, wout_ref, x1_ref, tile % tiles_per_seq, conv_done, ffn_done)
    next(ffn)
    next(mixers)
    next(mixers)
    next(ffn)
    next(ffn)
    for s in range(len(FFN_CHUNKS) - 2):
        for _ in range(CONV_PIECES_PER_CHUNK):
            next(mixers)
        next(ffn)
    next(ffn)
    next(mixers)


def _mix_ffn(a, up, uc, x, poolw, pscale, convw, convb, lng, lnb, pw, wout, g, wgu, wdown, seq):
    n = x.shape[0]
    tm = TOKEN_TILE
    nt = n // tm
    cur = lambda i: jnp.minimum(i, nt - 1)
    row = lambda c: pl.BlockSpec((tm, c), lambda i: (cur(i), 0))
    halo = lambda c: pl.BlockSpec(
        (HALO, c), lambda i: (jnp.maximum(cur(i) * (tm // HALO) - 1, 0), 0))
    return pl.pallas_call(
        functools.partial(_mix_ffn_kernel, tiles_per_seq=seq // tm, n_tiles=nt),
        grid=(nt + 1,),
        in_specs=[row(SB_WIDTH), row(POOL_WIDTH), halo(POOL_WIDTH), row(2 * CONV_WIDTH),
                  halo(2 * CONV_WIDTH), row(D_MODEL),
                  _resident((POOL_WIDTH, POOL_WIDTH)), _resident((1, POOL_WIDTH)),
                  _resident((CONV_KERNEL, CONV_WIDTH)), _resident((1, CONV_WIDTH)),
                  _resident((1, CONV_WIDTH)), _resident((1, CONV_WIDTH)),
                  _resident((CONV_WIDTH, CONV_WIDTH)), _resident((D_MODEL, D_MODEL)),
                  _resident((1, D_MODEL)), _resident((D_MODEL, 2 * FFN_HIDDEN)),
                  _resident((FFN_HIDDEN, D_MODEL))],
        out_specs=pl.BlockSpec((tm, D_MODEL), lambda i: (jnp.maximum(i - 1, 0), 0)),
        out_shape=jax.ShapeDtypeStruct((n, D_MODEL), F32),
        scratch_shapes=[pltpu.VMEM((tm, D_MODEL), F32)],
        compiler_params=_params("arbitrary"),
        name="mixers_ffn",
    )(a, up, up, uc, uc, x, poolw, pscale, convw, convb, lng, lnb, pw, wout, g, wgu, wdown)


def kernel(x, norm_mix_g, w_in, sb_q_g, sb_k_g, pool_w, pool_scale, conv_w, conv_b, conv_ln_g,
           conv_ln_b, conv_pw, w_out, norm_ffn_g, ffn_w_gu, ffn_w_down):
    batch, seq, d = x.shape
    assert d == D_MODEL and seq % TOKEN_TILE == 0 and seq % Q_BLOCK == 0
    xf = x.reshape(batch * seq, d)

    head_of = jnp.arange(MXU_TILE) // SB_HEAD_DIM
    headavg = jnp.where(head_of[:, None] == head_of[None, :], 1.0 / SB_HEAD_DIM, 0.0).astype(BF16)
    kidx = jnp.arange(K_BLOCK)
    tri = -(kidx[:, None] >= kidx[None, :]).astype(BF16)
    row = lambda v: v.reshape(1, -1).astype(F32)

    for l in range(DEPTH):
        poolw_bd = jax.scipy.linalg.block_diag(*[pool_w[l, g] for g in range(len(POOL_WINDOWS))])
        q, k, v, up, uc = _inproj(
            xf, row(norm_mix_g[l]), w_in[l].astype(BF16),
            row(jnp.tile(sb_q_g[l], SB_HEADS)), row(jnp.tile(sb_k_g[l], SB_HEADS)), headavg)
        a = _attention(_attention_skip_threshold(sb_q_g[l], sb_k_g[l]), q, k, v, tri, batch, seq)
        xf = _mix_ffn(a, up, uc, xf, poolw_bd.astype(BF16), row(pool_scale[l]),
                      conv_w[l].astype(F32), row(conv_b[l]), row(conv_ln_g[l]),
                      row(conv_ln_b[l]), conv_pw[l].astype(BF16), w_out[l].astype(BF16),
                      row(norm_ffn_g[l]), ffn_w_gu[l].astype(BF16), ffn_w_down[l].astype(BF16),
                      seq)
    return xf.reshape(batch, seq, d)
```

```python
import functools
import math

import jax
import jax.numpy as jnp
from jax import lax
from jax.experimental import pallas as pl
from jax.experimental.pallas import tpu as pltpu

F32 = jnp.float32
BF16 = jnp.bfloat16

D_MODEL = 1024
DEPTH = 4
SB_HEADS = 8
SB_HEAD_DIM = 64
SB_WIDTH = SB_HEADS * SB_HEAD_DIM
POOL_WINDOWS = (2, 4, 8, 16)
POOL_GROUP_DIM = 64
POOL_WIDTH = len(POOL_WINDOWS) * POOL_GROUP_DIM
CONV_WIDTH = D_MODEL - SB_WIDTH - POOL_WIDTH
CONV_KERNEL = 31
IN_COLS = 3 * SB_WIDTH + POOL_WIDTH + 2 * CONV_WIDTH
FFN_HIDDEN = int(math.ceil((8 * D_MODEL / 3) / 256) * 256)
RMS_EPS = 1e-6
LN_EPS = 1e-5
LOG2_E = math.log2(math.e)
SB_SCALE_LOG2 = LOG2_E / math.sqrt(SB_HEAD_DIM)
F32_EXP2_ZERO_BELOW = -152.0
LOGIT_BOUND_MARGIN = 1.05

V7X_VMEM_LIMIT_BYTES = 56 * 1024 * 1024
LANES = 128
SUBLANES = 8
MXU_TILE = 256

TOKEN_TILE = 512
HALO = 32
K_BLOCK = MXU_TILE
Q_BLOCK = 2 * K_BLOCK
CONV_ROW_BLOCK = 128
FFN_CHUNKS = ((0, 768), (768, 1536), (1536, 2304), (2304, FFN_HIDDEN))

assert HALO >= CONV_KERNEL - 1 and HALO >= max(POOL_WINDOWS) - 1 and HALO % SUBLANES == 0
assert all(lo % MXU_TILE == 0 and hi % MXU_TILE == 0 for lo, hi in FFN_CHUNKS)
CONV_PIECES_PER_CHUNK = 2
assert TOKEN_TILE // CONV_ROW_BLOCK == CONV_PIECES_PER_CHUNK * (len(FFN_CHUNKS) - 2)
_NT = (((1,), (1,)), ((), ()))
_NN = (((1,), (0,)), ((), ()))


def _params(*semantics):
    return pltpu.CompilerParams(dimension_semantics=semantics,
                                vmem_limit_bytes=V7X_VMEM_LIMIT_BYTES)


def _resident(shape):
    return pl.BlockSpec(shape, lambda *_: (0,) * len(shape), pipeline_mode=pl.Buffered(1))


def _rmsnorm(x, g):
    ms = jnp.mean(x * x, axis=-1, keepdims=True)
    return x * lax.rsqrt(ms + RMS_EPS) * g


def _inproj_kernel(x_ref, g_ref, w_ref, qg_ref, kg_ref, headavg_ref,
                   q_ref, k_ref, v_ref, up_ref, uc_ref):
    h = _rmsnorm(x_ref[...], g_ref[...]).astype(BF16)

    def proj(lo, hi):
        return jnp.dot(h, w_ref[:, lo:hi], preferred_element_type=F32)

    def head_norm(t, g):
        sq = (t * t).astype(BF16)
        ms = jnp.concatenate(
            [jnp.dot(sq[:, c:c + MXU_TILE], headavg_ref[...], preferred_element_type=F32)
             for c in range(0, SB_WIDTH, MXU_TILE)], axis=-1)
        return t * lax.rsqrt(ms + RMS_EPS) * g

    q_ref[...] = (head_norm(proj(0, SB_WIDTH), qg_ref[...]) * SB_SCALE_LOG2).astype(BF16)
    k_ref[...] = head_norm(proj(SB_WIDTH, 2 * SB_WIDTH), kg_ref[...]).astype(BF16)
    v_ref[...] = proj(2 * SB_WIDTH, 3 * SB_WIDTH).astype(BF16)
    up_ref[...] = proj(3 * SB_WIDTH, 3 * SB_WIDTH + POOL_WIDTH)
    uc_ref[...] = proj(3 * SB_WIDTH + POOL_WIDTH, IN_COLS)


def _inproj(x, g, w, qg, kg, headavg):
    n = x.shape[0]
    tm = TOKEN_TILE
    row = lambda c: pl.BlockSpec((tm, c), lambda i: (i, 0))
    return pl.pallas_call(
        _inproj_kernel,
        grid=(n // tm,),
        in_specs=[row(D_MODEL), _resident((1, D_MODEL)), _resident((D_MODEL, IN_COLS)),
                  _resident((1, SB_WIDTH)), _resident((1, SB_WIDTH)),
                  _resident((MXU_TILE, MXU_TILE))],
        out_specs=[row(SB_WIDTH), row(SB_WIDTH), row(SB_WIDTH), row(POOL_WIDTH),
                   row(2 * CONV_WIDTH)],
        out_shape=[jax.ShapeDtypeStruct((n, SB_WIDTH), BF16)] * 3
        + [jax.ShapeDtypeStruct((n, POOL_WIDTH), F32),
           jax.ShapeDtypeStruct((n, 2 * CONV_WIDTH), F32)],
        compiler_params=_params("parallel"),
        name="inproj",
    )(x, g, w, qg, kg, headavg)


def _attn_kernel(skip_below_ref, q_ref, k_ref, v_ref, tri_ref, o_ref, acc_ref, carry_ref):
    seq = q_ref.shape[0]
    half = Q_BLOCK // 2
    head_lanes = [slice(h * SB_HEAD_DIM, (h + 1) * SB_HEAD_DIM) for h in range(SB_HEADS)]
    pair_lanes = [slice(p * LANES, (p + 1) * LANES) for p in range(SB_HEADS // 2)]
    lane_is_first = lax.broadcasted_iota(jnp.int32, (K_BLOCK, LANES), 1) < SB_HEAD_DIM

    def strict(rows):
        return (lax.broadcasted_iota(jnp.int32, (rows, K_BLOCK), 1)
                < lax.broadcasted_iota(jnp.int32, (rows, K_BLOCK), 0))

    def lane_bcast(col):
        return jnp.broadcast_to(col, (col.shape[0], LANES))

    def tile(qh, kt, carry, mask):
        z = lax.dot_general(qh, kt, _NT, preferred_element_type=F32)
        neg_abs = lax.bitcast_convert_type(
            lax.bitcast_convert_type(z, jnp.uint32) | jnp.uint32(0x80000000), F32)
        softplus = jnp.maximum(z, 0.0) + jnp.log(1.0 + jnp.exp2(neg_abs)) * LOG2_E
        if mask is not None:
            softplus = jnp.where(mask, softplus, 0.0)
        incl = lax.dot_general(softplus, tri_ref[...], _NN, preferred_element_type=F32)
        expo = z + incl
        if carry is not None:
            expo = expo + jnp.concatenate([carry] * (K_BLOCK // LANES), axis=1)
        w = jnp.exp2(jnp.minimum(expo, 0.0))
        if mask is not None:
            w = jnp.where(mask, w, 0.0)
        if mask is None:
            new_carry = lane_bcast(expo[:, 0:1] - z[:, 0:1])
        else:
            new_carry = lane_bcast(incl[:, 0:1])
            if carry is not None:
                new_carry = new_carry + carry
        return w, new_carry

    def pair_pv(w_first, w_second, v_pair):
        zero = jnp.zeros_like(v_pair)
        v_both = jnp.concatenate([jnp.where(lane_is_first, v_pair, zero),
                                  jnp.where(lane_is_first, zero, v_pair)], axis=0)
        return lax.dot_general(jnp.concatenate([w_first, w_second], axis=1), v_both, _NN,
                               preferred_element_type=F32)

    def q_block(qi, _):
        r0 = pl.multiple_of(qi * Q_BLOCK, Q_BLOCK)
        r1 = pl.multiple_of(qi * Q_BLOCK + half, half)
        halves = (slice(0, half), slice(half, Q_BLOCK))
        carry_max = [None, None]
        bottoms = [tile(q_ref[pl.ds(r1, half), lanes], k_ref[pl.ds(r1, K_BLOCK), lanes], None,
                        strict(half)) for lanes in head_lanes]
        talls = []
        for h, lanes in enumerate(head_lanes):
            carry0 = jnp.concatenate([jnp.zeros((half, LANES), F32), bottoms[h][1]], axis=0)
            w, carry = tile(q_ref[pl.ds(r0, Q_BLOCK), lanes],
                            k_ref[pl.ds(r0, K_BLOCK), lanes], carry0, strict(Q_BLOCK))
            carry_ref[h] = carry
            carry_max = [carry[rows] if m is None else jnp.maximum(m, carry[rows])
                         for m, rows in zip(carry_max, halves)]
            talls.append(w)
        for p, pair in enumerate(pair_lanes):
            pv_b = pair_pv(bottoms[2 * p][0], bottoms[2 * p + 1][0],
                           v_ref[pl.ds(r1, K_BLOCK), pair])
            pv = pair_pv(talls[2 * p], talls[2 * p + 1], v_ref[pl.ds(r0, K_BLOCK), pair])
            acc_ref[p] = pv + jnp.concatenate([jnp.zeros((half, LANES), F32), pv_b], axis=0)

        def more_tiles(state):
            jj, worst_carry = state
            return jnp.logical_and(jj < 2 * qi, worst_carry > skip_below_ref[0])

        for rows, worst in zip(halves, carry_max):
            def k_tile(state, rows=rows):
                jj, _ = state
                c0 = pl.multiple_of((2 * qi - 1 - jj) * K_BLOCK, K_BLOCK)
                worst = None
                ws = []
                for h, lanes in enumerate(head_lanes):
                    w, carry = tile(q_ref[pl.ds(r0 + rows.start, half), lanes],
                                    k_ref[pl.ds(c0, K_BLOCK), lanes], carry_ref[h, rows], None)
                    carry_ref[h, rows] = carry
                    worst = carry if worst is None else jnp.maximum(worst, carry)
                    ws.append(w)
                for p, pair in enumerate(pair_lanes):
                    acc_ref[p, rows] += pair_pv(ws[2 * p], ws[2 * p + 1],
                                                v_ref[pl.ds(c0, K_BLOCK), pair])
                return jj + 1, jnp.max(worst)

            lax.while_loop(more_tiles, k_tile, (jnp.int32(0), jnp.max(worst)))

        o_ref[pl.ds(r0, Q_BLOCK), :] = jnp.concatenate(
            [acc_ref[p] for p in range(len(pair_lanes))], axis=1).astype(o_ref.dtype)
        return 0

    lax.fori_loop(0, seq // Q_BLOCK, q_block, 0)


def _attention_skip_threshold(q_gain, k_gain):
    logit_bound = (SB_HEAD_DIM * SB_SCALE_LOG2 * LOGIT_BOUND_MARGIN
                   * jnp.max(jnp.abs(q_gain)) * jnp.max(jnp.abs(k_gain)))
    return (F32_EXP2_ZERO_BELOW - logit_bound).reshape(1).astype(F32)


def _attention(skip_below, q, k, v, tri, batch, seq):
    blk = pl.BlockSpec((seq, SB_WIDTH), lambda b: (b, 0))
    return pl.pallas_call(
        _attn_kernel,
        grid=(batch,),
        in_specs=[pl.BlockSpec(memory_space=pltpu.SMEM), blk, blk, blk,
                  _resident((K_BLOCK, K_BLOCK))],
        out_specs=blk,
        out_shape=jax.ShapeDtypeStruct((batch * seq, SB_WIDTH), BF16),
        scratch_shapes=[pltpu.VMEM((SB_HEADS // 2, Q_BLOCK, LANES), F32),
                        pltpu.VMEM((SB_HEADS, Q_BLOCK, LANES), F32)],
        compiler_params=_params("parallel"),
        name="sb_attention",
    )(skip_below, q, k, v, tri)


def _zero_after(v):
    bits = lax.bitcast_convert_type(v[0:1, 0:1], jnp.uint32)
    return ((bits >> 16) >> 16).astype(F32)


def _mixers_steps(a_ref, up_ref, up_halo_ref, uc_ref, uc_halo_ref, x_ref,
                  poolw_ref, pscale_ref, convw_ref, convb_ref, lng_ref, lnb_ref, pw_ref,
                  wout_ref, x1_ref, tile_in_seq, conv_done, ffn_done):
    tm = up_ref.shape[0]
    rows = tm + HALO
    has_prev = tile_in_seq > 0

    u = jnp.concatenate([jnp.where(has_prev, up_halo_ref[...], 0.0), up_ref[...]], axis=0)
    s2 = u + pltpu.roll(u, 1, 0)
    s4 = s2 + pltpu.roll(s2, 2, 0)
    s8 = s4 + pltpu.roll(s4, 4, 0)
    s16 = s8 + pltpu.roll(s8, 8, 0)
    group = lax.broadcasted_iota(jnp.int32, (tm, POOL_WIDTH), 1) // POOL_GROUP_DIM
    win_sum = jnp.where(group == 0, s2[HALO:],
                        jnp.where(group == 1, s4[HALO:],
                                  jnp.where(group == 2, s8[HALO:], s16[HALO:])))
    window = jnp.where(group == 0, float(POOL_WINDOWS[0]),
                       jnp.where(group == 1, float(POOL_WINDOWS[1]),
                                 jnp.where(group == 2, float(POOL_WINDOWS[2]),
                                           float(POOL_WINDOWS[3]))))
    pos = (tile_in_seq * tm + lax.broadcasted_iota(jnp.int32, (tm, POOL_WIDTH), 0)).astype(F32)
    pooled = win_sum / jnp.minimum(pos + 1.0, window) - u[HALO:]
    p_out = jnp.dot(pooled.astype(BF16), poolw_ref[...], preferred_element_type=F32)
    p_out = (p_out * pscale_ref[...]).astype(BF16)
    yield

    uc = jnp.concatenate([jnp.where(has_prev, uc_halo_ref[...], 0.0), uc_ref[...]], axis=0)
    glu = uc[:, :CONV_WIDTH] * jax.nn.sigmoid(uc[:, CONV_WIDTH:])
    shifted = [glu] + [pltpu.roll(glu, rows - r, 0) for r in range(1, SUBLANES)]
    yield
    first_tap = HALO - (CONV_KERNEL - 1)
    ys = []
    for r0 in range(0, tm, CONV_ROW_BLOCK):
        conv = None
        for kk in range(CONV_KERNEL):
            r, base = (first_tap + kk) % SUBLANES, (first_tap + kk) // SUBLANES * SUBLANES
            term = shifted[r][base + r0:base + r0 + CONV_ROW_BLOCK] * convw_ref[kk:kk + 1, :]
            conv = term if conv is None else conv + term
        conv = conv + (convb_ref[...] + ffn_done[(r0 // CONV_ROW_BLOCK) // CONV_PIECES_PER_CHUNK])
        mu = jnp.mean(conv, axis=-1, keepdims=True)
        cen = conv - mu
        var = jnp.mean(cen * cen, axis=-1, keepdims=True)
        y = cen * lax.rsqrt(var + LN_EPS) * lng_ref[...] + lnb_ref[...]
        y = y * jax.nn.sigmoid(y)
        conv_done.append(_zero_after(y))
        ys.append(y.astype(BF16))
        yield
    c_out = jnp.dot(jnp.concatenate(ys, axis=0), pw_ref[...], preferred_element_type=F32)

    mix = jnp.concatenate([a_ref[...], p_out, c_out.astype(BF16)], axis=-1)
    x1_ref[...] = x_ref[...] + jnp.dot(mix, wout_ref[...], preferred_element_type=F32)
    yield


def _ffn_steps(x1, g_ref, wgu_ref, wdown_ref, o_ref, conv_done, ffn_done):
    h = _rmsnorm(x1, g_ref[...]).astype(BF16)
    yield
    acc = x1
    for lo, hi in FFN_CHUNKS:
        hc = h if not conv_done else h + conv_done[-1].astype(BF16)
        gate = jnp.dot(hc, wgu_ref[:, lo:hi], preferred_element_type=F32)
        up = jnp.dot(hc, wgu_ref[:, FFN_HIDDEN + lo:FFN_HIDDEN + hi], preferred_element_type=F32)
        act = (gate * jax.nn.sigmoid(gate) * up).astype(BF16)
        acc = acc + jnp.dot(act, wdown_ref[lo:hi, :], preferred_element_type=F32)
        ffn_done.append(_zero_after(acc))
        yield
    o_ref[...] = acc
    yield


def _mix_ffn_kernel(a_ref, up_ref, up_halo_ref, uc_ref, uc_halo_ref, x_ref,
                    poolw_ref, pscale_ref, convw_ref, convb_ref, lng_ref, lnb_ref, pw_ref,
                    wout_ref, g_ref, wgu_ref, wdown_ref, o_ref, x1_ref, *, tiles_per_seq, n_tiles):
    i = pl.program_id(0)

    @pl.when(i == 0)
    def _():
        x1_ref[...] = jnp.zeros_like(x1_ref)

    tile = jnp.minimum(i, n_tiles - 1)
    conv_done, ffn_done = [], []
    ffn = _ffn_steps(x1_ref[...], g_ref, wgu_ref, wdown_ref, o_ref, conv_done, ffn_done)
    mixers = _mixers_steps(a_ref, up_ref, up_halo_ref, uc_ref, uc_halo_ref, x_ref,
                           poolw_ref, pscale_ref, convw_ref, convb_ref, lng_ref, lnb_ref,
                           pw_ref, wout_ref, x1_ref, tile % tiles_per_seq, conv_done, ffn_done)
    next(ffn)
    next(mixers)
    next(mixers)
    next(ffn)
    next(ffn)
    for s in range(len(FFN_CHUNKS) - 2):
        for _ in range(CONV_PIECES_PER_CHUNK):
            next(mixers)
        next(ffn)
    next(ffn)
    next(mixers)


def _mix_ffn(a, up, uc, x, poolw, pscale, convw, convb, lng, lnb, pw, wout, g, wgu, wdown, seq):
    n = x.shape[0]
    tm = TOKEN_TILE
    nt = n // tm
    cur = lambda i: jnp.minimum(i, nt - 1)
    row = lambda c: pl.BlockSpec((tm, c), lambda i: (cur(i), 0))
    halo = lambda c: pl.BlockSpec(
        (HALO, c), lambda i: (jnp.maximum(cur(i) * (tm // HALO) - 1, 0), 0))
    return pl.pallas_call(
        functools.partial(_mix_ffn_kernel, tiles_per_seq=seq // tm, n_tiles=nt),
        grid=(nt + 1,),
        in_specs=[row(SB_WIDTH), row(POOL_WIDTH), halo(POOL_WIDTH), row(2 * CONV_WIDTH),
                  halo(2 * CONV_WIDTH), row(D_MODEL),
                  _resident((POOL_WIDTH, POOL_WIDTH)), _resident((1, POOL_WIDTH)),
                  _resident((CONV_KERNEL, CONV_WIDTH)), _resident((1, CONV_WIDTH)),
                  _resident((1, CONV_WIDTH)), _resident((1, CONV_WIDTH)),
                  _resident((CONV_WIDTH, CONV_WIDTH)), _resident((D_MODEL, D_MODEL)),
                  _resident((1, D_MODEL)), _resident((D_MODEL, 2 * FFN_HIDDEN)),
                  _resident((FFN_HIDDEN, D_MODEL))],
        out_specs=pl.BlockSpec((tm, D_MODEL), lambda i: (jnp.maximum(i - 1, 0), 0)),
        out_shape=jax.ShapeDtypeStruct((n, D_MODEL), F32),
        scratch_shapes=[pltpu.VMEM((tm, D_MODEL), F32)],
        compiler_params=_params("arbitrary"),
        name="mixers_ffn",
    )(a, up, up, uc, uc, x, poolw, pscale, convw, convb, lng, lnb, pw, wout, g, wgu, wdown)


def kernel(x, norm_mix_g, w_in, sb_q_g, sb_k_g, pool_w, pool_scale, conv_w, conv_b, conv_ln_g,
           conv_ln_b, conv_pw, w_out, norm_ffn_g, ffn_w_gu, ffn_w_down):
    batch, seq, d = x.shape
    assert d == D_MODEL and seq % TOKEN_TILE == 0 and seq % Q_BLOCK == 0
    xf = x.reshape(batch * seq, d)

    head_of = jnp.arange(MXU_TILE) // SB_HEAD_DIM
    headavg = jnp.where(head_of[:, None] == head_of[None, :], 1.0 / SB_HEAD_DIM, 0.0).astype(BF16)
    kidx = jnp.arange(K_BLOCK)
    tri = -(kidx[:, None] >= kidx[None, :]).astype(BF16)
    row = lambda v: v.reshape(1, -1).astype(F32)

    for l in range(DEPTH):
        poolw_bd = jax.scipy.linalg.block_diag(*[pool_w[l, g] for g in range(len(POOL_WINDOWS))])
        q, k, v, up, uc = _inproj(
            xf, row(norm_mix_g[l]), w_in[l].astype(BF16),
            row(jnp.tile(sb_q_g[l], SB_HEADS)), row(jnp.tile(sb_k_g[l], SB_HEADS)), headavg)
        a = _attention(_attention_skip_threshold(sb_q_g[l], sb_k_g[l]), q, k, v, tri, batch, seq)
        xf = _mix_ffn(a, up, uc, xf, poolw_bd.astype(BF16), row(pool_scale[l]),
                      conv_w[l].astype(F32), row(conv_b[l]), row(conv_ln_g[l]),
                      row(conv_ln_b[l]), conv_pw[l].astype(BF16), w_out[l].astype(BF16),
                      row(norm_ffn_g[l]), ffn_w_gu[l].astype(BF16), ffn_w_down[l].astype(BF16),
                      seq)
    return xf.reshape(batch, seq, d)
```

```python
import functools
import math

import jax
import jax.numpy as jnp
from jax import lax
from jax.experimental import pallas as pl
from jax.experimental.pallas import tpu as pltpu

F32 = jnp.float32
BF16 = jnp.bfloat16

D_MODEL = 1024
DEPTH = 4
SB_HEADS = 8
SB_HEAD_DIM = 64
SB_WIDTH = SB_HEADS * SB_HEAD_DIM
POOL_WINDOWS = (2, 4, 8, 16)
POOL_GROUP_DIM = 64
POOL_WIDTH = len(POOL_WINDOWS) * POOL_GROUP_DIM
CONV_WIDTH = D_MODEL - SB_WIDTH - POOL_WIDTH
CONV_KERNEL = 31
IN_COLS = 3 * SB_WIDTH + POOL_WIDTH + 2 * CONV_WIDTH
FFN_HIDDEN = int(math.ceil((8 * D_MODEL / 3) / 256) * 256)
RMS_EPS = 1e-6
LN_EPS = 1e-5
LOG2_E = math.log2(math.e)
SB_SCALE_LOG2 = LOG2_E / math.sqrt(SB_HEAD_DIM)
F32_EXP2_ZERO_BELOW = -152.0
LOGIT_BOUND_MARGIN = 1.05

V7X_VMEM_LIMIT_BYTES = 56 * 1024 * 1024
LANES = 128
SUBLANES = 8
MXU_TILE = 256

TOKEN_TILE = 512
HALO = 32
K_BLOCK = MXU_TILE
Q_BLOCK = 2 * K_BLOCK
CONV_ROW_BLOCK = 128
FFN_CHUNKS = ((0, 768), (768, 1536), (1536, 2304), (2304, FFN_HIDDEN))

assert HALO >= CONV_KERNEL - 1 and HALO >= max(POOL_WINDOWS) - 1 and HALO % SUBLANES == 0
assert all(lo % MXU_TILE == 0 and hi % MXU_TILE == 0 for lo, hi in FFN_CHUNKS)
CONV_PIECES_PER_CHUNK = 2
assert TOKEN_TILE // CONV_ROW_BLOCK == CONV_PIECES_PER_CHUNK * (len(FFN_CHUNKS) - 2)
_NT = (((1,), (1,)), ((), ()))
_NN = (((1,), (0,)), ((), ()))


def _params(*semantics):
    return pltpu.CompilerParams(dimension_semantics=semantics,
                                vmem_limit_bytes=V7X_VMEM_LIMIT_BYTES)


def _resident(shape):
    return pl.BlockSpec(shape, lambda *_: (0,) * len(shape), pipeline_mode=pl.Buffered(1))


def _rmsnorm(x, g):
    ms = jnp.mean(x * x, axis=-1, keepdims=True)
    return x * lax.rsqrt(ms + RMS_EPS) * g


def _inproj_kernel(x_ref, g_ref, w_ref, qg_ref, kg_ref, headavg_ref,
                   q_ref, k_ref, v_ref, up_ref, uc_ref):
    h = _rmsnorm(x_ref[...], g_ref[...]).astype(BF16)

    def proj(lo, hi):
        return jnp.dot(h, w_ref[:, lo:hi], preferred_element_type=F32)

    def head_norm(t, g):
        sq = (t * t).astype(BF16)
        ms = jnp.concatenate(
            [jnp.dot(sq[:, c:c + MXU_TILE], headavg_ref[...], preferred_element_type=F32)
             for c in range(0, SB_WIDTH, MXU_TILE)], axis=-1)
        return t * lax.rsqrt(ms + RMS_EPS) * g

    q_ref[...] = (head_norm(proj(0, SB_WIDTH), qg_ref[...]) * SB_SCALE_LOG2).astype(BF16)
    k_ref[...] = head_norm(proj(SB_WIDTH, 2 * SB_WIDTH), kg_ref[...]).astype(BF16)
    v_ref[...] = proj(2 * SB_WIDTH, 3 * SB_WIDTH).astype(BF16)
    up_ref[...] = proj(3 * SB_WIDTH, 3 * SB_WIDTH + POOL_WIDTH)
    uc_ref[...] = proj(3 * SB_WIDTH + POOL_WIDTH, IN_COLS)


def _inproj(x, g, w, qg, kg, headavg):
    n = x.shape[0]
    tm = TOKEN_TILE
    row = lambda c: pl.BlockSpec((tm, c), lambda i: (i, 0))
    return pl.pallas_call(
        _inproj_kernel,
        grid=(n // tm,),
        in_specs=[row(D_MODEL), _resident((1, D_MODEL)), _resident((D_MODEL, IN_COLS)),
                  _resident((1, SB_WIDTH)), _resident((1, SB_WIDTH)),
                  _resident((MXU_TILE, MXU_TILE))],
        out_specs=[row(SB_WIDTH), row(SB_WIDTH), row(SB_WIDTH), row(POOL_WIDTH),
                   row(2 * CONV_WIDTH)],
        out_shape=[jax.ShapeDtypeStruct((n, SB_WIDTH), BF16)] * 3
        + [jax.ShapeDtypeStruct((n, POOL_WIDTH), F32),
           jax.ShapeDtypeStruct((n, 2 * CONV_WIDTH), F32)],
        compiler_params=_params("parallel"),
        name="inproj",
    )(x, g, w, qg, kg, headavg)


def _attn_kernel(skip_below_ref, q_ref, k_ref, v_ref, tri_ref, o_ref, acc_ref, carry_ref):
    seq = q_ref.shape[0]
    half = Q_BLOCK // 2
    head_lanes = [slice(h * SB_HEAD_DIM, (h + 1) * SB_HEAD_DIM) for h in range(SB_HEADS)]
    pair_lanes = [slice(p * LANES, (p + 1) * LANES) for p in range(SB_HEADS // 2)]
    lane_is_first = lax.broadcasted_iota(jnp.int32, (K_BLOCK, LANES), 1) < SB_HEAD_DIM

    def strict(rows):
        return (lax.broadcasted_iota(jnp.int32, (rows, K_BLOCK), 1)
                < lax.broadcasted_iota(jnp.int32, (rows, K_BLOCK), 0))

    def lane_bcast(col):
        return jnp.broadcast_to(col, (col.shape[0], LANES))

    def tile(qh, kt, carry, mask):
        z = lax.dot_general(qh, kt, _NT, preferred_element_type=F32)
        neg_abs = lax.bitcast_convert_type(
            lax.bitcast_convert_type(z, jnp.uint32) | jnp.uint32(0x80000000), F32)
        softplus = jnp.maximum(z, 0.0) + jnp.log(1.0 + jnp.exp2(neg_abs)) * LOG2_E
        if mask is not None:
            softplus = jnp.where(mask, softplus, 0.0)
        incl = lax.dot_general(softplus, tri_ref[...], _NN, preferred_element_type=F32)
        expo = z + incl
        if carry is not None:
            expo = expo + jnp.concatenate([carry] * (K_BLOCK // LANES), axis=1)
        w = jnp.exp2(jnp.minimum(expo, 0.0))
        if mask is not None:
            w = jnp.where(mask, w, 0.0)
        if mask is None:
            new_carry = lane_bcast(expo[:, 0:1] - z[:, 0:1])
        else:
            new_carry = lane_bcast(incl[:, 0:1])
            if carry is not None:
                new_carry = new_carry + carry
        return w, new_carry

    def pair_pv(w_first, w_second, v_pair):
        zero = jnp.zeros_like(v_pair)
        v_both = jnp.concatenate([jnp.where(lane_is_first, v_pair, zero),
                                  jnp.where(lane_is_first, zero, v_pair)], axis=0)
        return lax.dot_general(jnp.concatenate([w_first, w_second], axis=1), v_both, _NN,
                               preferred_element_type=F32)

    def q_block(qi, _):
        r0 = pl.multiple_of(qi * Q_BLOCK, Q_BLOCK)
        r1 = pl.multiple_of(qi * Q_BLOCK + half, half)
        halves = (slice(0, half), slice(half, Q_BLOCK))
        bottom_worst = None
        bottoms = [tile(q_ref[pl.ds(r1, half), lanes], k_ref[pl.ds(r1, K_BLOCK), lanes], None,
                        strict(half)) for lanes in head_lanes]
        talls = []
        for h, lanes in enumerate(head_lanes):
            carry0 = jnp.concatenate([jnp.zeros((half, LANES), F32), bottoms[h][1]], axis=0)
            w, carry = tile(q_ref[pl.ds(r0, Q_BLOCK), lanes],
                            k_ref[pl.ds(r0, K_BLOCK), lanes], carry0, strict(Q_BLOCK))
            carry_ref[h] = carry
            bottom_worst = (carry[half:] if bottom_worst is None
                            else jnp.maximum(bottom_worst, carry[half:]))
            talls.append(w)
        for p, pair in enumerate(pair_lanes):
            pv_b = pair_pv(bottoms[2 * p][0], bottoms[2 * p + 1][0],
                           v_ref[pl.ds(r1, K_BLOCK), pair])
            pv = pair_pv(talls[2 * p], talls[2 * p + 1], v_ref[pl.ds(r0, K_BLOCK), pair])
            acc_ref[p] = pv + jnp.concatenate([jnp.zeros((half, LANES), F32), pv_b], axis=0)

        def more_tiles(state):
            jj, worst_carry = state
            return jnp.logical_and(jj < 2 * qi, worst_carry > skip_below_ref[0])

        for rows, worst in zip(halves, (jnp.zeros((1, 1), F32), bottom_worst)):
            def k_tile(state, rows=rows):
                jj, _ = state
                c0 = pl.multiple_of((2 * qi - 1 - jj) * K_BLOCK, K_BLOCK)
                worst = None
                ws = []
                for h, lanes in enumerate(head_lanes):
                    w, carry = tile(q_ref[pl.ds(r0 + rows.start, half), lanes],
                                    k_ref[pl.ds(c0, K_BLOCK), lanes], carry_ref[h, rows], None)
                    carry_ref[h, rows] = carry
                    worst = carry if worst is None else jnp.maximum(worst, carry)
                    ws.append(w)
                for p, pair in enumerate(pair_lanes):
                    acc_ref[p, rows] += pair_pv(ws[2 * p], ws[2 * p + 1],
                                                v_ref[pl.ds(c0, K_BLOCK), pair])
                return jj + 1, jnp.max(worst)

            lax.while_loop(more_tiles, k_tile, (jnp.int32(0), jnp.max(worst)))

        o_ref[pl.ds(r0, Q_BLOCK), :] = jnp.concatenate(
            [acc_ref[p] for p in range(len(pair_lanes))], axis=1).astype(o_ref.dtype)
        return 0

    lax.fori_loop(0, seq // Q_BLOCK, q_block, 0)


def _attention_skip_threshold(q_gain, k_gain):
    logit_bound = (SB_HEAD_DIM * SB_SCALE_LOG2 * LOGIT_BOUND_MARGIN
                   * jnp.max(jnp.abs(q_gain)) * jnp.max(jnp.abs(k_gain)))
    return (F32_EXP2_ZERO_BELOW - logit_bound).reshape(1).astype(F32)


def _attention(skip_below, q, k, v, tri, batch, seq):
    blk = pl.BlockSpec((seq, SB_WIDTH), lambda b: (b, 0))
    return pl.pallas_call(
        _attn_kernel,
        grid=(batch,),
        in_specs=[pl.BlockSpec(memory_space=pltpu.SMEM), blk, blk, blk,
                  _resident((K_BLOCK, K_BLOCK))],
        out_specs=blk,
        out_shape=jax.ShapeDtypeStruct((batch * seq, SB_WIDTH), BF16),
        scratch_shapes=[pltpu.VMEM((SB_HEADS // 2, Q_BLOCK, LANES), F32),
                        pltpu.VMEM((SB_HEADS, Q_BLOCK, LANES), F32)],
        compiler_params=_params("parallel"),
        name="sb_attention",
    )(skip_below, q, k, v, tri)


def _zero_after(v):
    bits = lax.bitcast_convert_type(v[0:1, 0:1], jnp.uint32)
    return ((bits >> 16) >> 16).astype(F32)


def _mixers_steps(a_ref, up_ref, up_halo_ref, uc_ref, uc_halo_ref, x_ref,
                  poolw_ref, pscale_ref, convw_ref, convb_ref, lng_ref, lnb_ref, pw_ref,
                  wout_ref, x1_ref, tile_in_seq, conv_done, ffn_done):
    tm = up_ref.shape[0]
    rows = tm + HALO
    has_prev = tile_in_seq > 0

    u = jnp.concatenate([jnp.where(has_prev, up_halo_ref[...], 0.0), up_ref[...]], axis=0)
    s2 = u + pltpu.roll(u, 1, 0)
    s4 = s2 + pltpu.roll(s2, 2, 0)
    s8 = s4 + pltpu.roll(s4, 4, 0)
    s16 = s8 + pltpu.roll(s8, 8, 0)
    group = lax.broadcasted_iota(jnp.int32, (tm, POOL_WIDTH), 1) // POOL_GROUP_DIM
    win_sum = jnp.where(group == 0, s2[HALO:],
                        jnp.where(group == 1, s4[HALO:],
                                  jnp.where(group == 2, s8[HALO:], s16[HALO:])))
    window = jnp.where(group == 0, float(POOL_WINDOWS[0]),
                       jnp.where(group == 1, float(POOL_WINDOWS[1]),
                                 jnp.where(group == 2, float(POOL_WINDOWS[2]),
                                           float(POOL_WINDOWS[3]))))
    pos = (tile_in_seq * tm + lax.broadcasted_iota(jnp.int32, (tm, POOL_WIDTH), 0)).astype(F32)
    pooled = win_sum / jnp.minimum(pos + 1.0, window) - u[HALO:]
    p_out = jnp.dot(pooled.astype(BF16), poolw_ref[...], preferred_element_type=F32)
    p_out = (p_out * pscale_ref[...]).astype(BF16)
    yield

    uc = jnp.concatenate([jnp.where(has_prev, uc_halo_ref[...], 0.0), uc_ref[...]], axis=0)
    glu = uc[:, :CONV_WIDTH] * jax.nn.sigmoid(uc[:, CONV_WIDTH:])
    shifted = [glu] + [pltpu.roll(glu, rows - r, 0) for r in range(1, SUBLANES)]
    yield
    first_tap = HALO - (CONV_KERNEL - 1)
    ys = []
    for r0 in range(0, tm, CONV_ROW_BLOCK):
        conv = None
        for kk in range(CONV_KERNEL):
            r, base = (first_tap + kk) % SUBLANES, (first_tap + kk) // SUBLANES * SUBLANES
            term = shifted[r][base + r0:base + r0 + CONV_ROW_BLOCK] * convw_ref[kk:kk + 1, :]
            conv = term if conv is None else conv + term
        conv = conv + (convb_ref[...] + ffn_done[(r0 // CONV_ROW_BLOCK) // CONV_PIECES_PER_CHUNK])
        mu = jnp.mean(conv, axis=-1, keepdims=True)
        cen = conv - mu
        var = jnp.mean(cen * cen, axis=-1, keepdims=True)
        y = cen * lax.rsqrt(var + LN_EPS) * lng_ref[...] + lnb_ref[...]
        y = y * jax.nn.sigmoid(y)
        conv_done.append(_zero_after(y))
        ys.append(y.astype(BF16))
        yield
    c_out = jnp.dot(jnp.concatenate(ys, axis=0), pw_ref[...], preferred_element_type=F32)

    mix = jnp.concatenate([a_ref[...], p_out, c_out.astype(BF16)], axis=-1)
    x1_ref[...] = x_ref[...] + jnp.dot(mix, wout_ref[...], preferred_element_type=F32)
    yield


def _ffn_steps(x1, g_ref, wgu_ref, wdown_ref, o_ref, conv_done, ffn_done):
    h = _rmsnorm(x1, g_ref[...]).astype(BF16)
    yield
    acc = x1
    for lo, hi in FFN_CHUNKS:
        hc = h if not conv_done else h + conv_done[-1].astype(BF16)
        gate = jnp.dot(hc, wgu_ref[:, lo:hi], preferred_element_type=F32)
        up = jnp.dot(hc, wgu_ref[:, FFN_HIDDEN + lo:FFN_HIDDEN + hi], preferred_element_type=F32)
        act = (gate * jax.nn.sigmoid(gate) * up).astype(BF16)
        acc = acc + jnp.dot(act, wdown_ref[lo:hi, :], preferred_element_type=F32)
        ffn_done.append(_zero_after(acc))
        yield
    o_ref[...] = acc
    yield


def _mix_ffn_kernel(a_ref, up_ref, up_halo_ref, uc_ref, uc_halo_ref, x_ref,
                    poolw_ref, pscale_ref, convw_ref, convb_ref, lng_ref, lnb_ref, pw_ref,
                    wout_ref, g_ref, wgu_ref, wdown_ref, o_ref, x1_ref, *, tiles_per_seq, n_tiles):
    i = pl.program_id(0)

    @pl.when(i == 0)
    def _():
        x1_ref[...] = jnp.zeros_like(x1_ref)

    tile = jnp.minimum(i, n_tiles - 1)
    conv_done, ffn_done = [], []
    ffn = _ffn_steps(x1_ref[...], g_ref, wgu_ref, wdown_ref, o_ref, conv_done, ffn_done)
    mixers = _mixers_steps(a_ref, up_ref, up_halo_ref, uc_ref, uc_halo_ref, x_ref,
                           poolw_ref, pscale_ref, convw_ref, convb_ref, lng_ref, lnb_ref,
                           pw_ref, wout_ref, x1_ref, tile % tiles_per_seq, conv_done, ffn_done)
    next(ffn)
    next(mixers)
    next(mixers)
    next(ffn)
    next(ffn)
    for s in range(len(FFN_CHUNKS) - 2):
        for _ in range(CONV_PIECES_PER_CHUNK):
            next(mixers)
        next(ffn)
    next(ffn)
    next(mixers)


def _mix_ffn(a, up, uc, x, poolw, pscale, convw, convb, lng, lnb, pw, wout, g, wgu, wdown, seq):
    n = x.shape[0]
    tm = TOKEN_TILE
    nt = n // tm
    cur = lambda i: jnp.minimum(i, nt - 1)
    row = lambda c: pl.BlockSpec((tm, c), lambda i: (cur(i), 0))
    halo = lambda c: pl.BlockSpec(
        (HALO, c), lambda i: (jnp.maximum(cur(i) * (tm // HALO) - 1, 0), 0))
    return pl.pallas_call(
        functools.partial(_mix_ffn_kernel, tiles_per_seq=seq // tm, n_tiles=nt),
        grid=(nt + 1,),
        in_specs=[row(SB_WIDTH), row(POOL_WIDTH), halo(POOL_WIDTH), row(2 * CONV_WIDTH),
                  halo(2 * CONV_WIDTH), row(D_MODEL),
                  _resident((POOL_WIDTH, POOL_WIDTH)), _resident((1, POOL_WIDTH)),
                  _resident((CONV_KERNEL, CONV_WIDTH)), _resident((1, CONV_WIDTH)),
                  _resident((1, CONV_WIDTH)), _resident((1, CONV_WIDTH)),
                  _resident((CONV_WIDTH, CONV_WIDTH)), _resident((D_MODEL, D_MODEL)),
                  _resident((1, D_MODEL)), _resident((D_MODEL, 2 * FFN_HIDDEN)),
                  _resident((FFN_HIDDEN, D_MODEL))],
        out_specs=pl.BlockSpec((tm, D_MODEL), lambda i: (jnp.maximum(i - 1, 0), 0)),
        out_shape=jax.ShapeDtypeStruct((n, D_MODEL), F32),
        scratch_shapes=[pltpu.VMEM((tm, D_MODEL), F32)],
        compiler_params=_params("arbitrary"),
        name="mixers_ffn",
    )(a, up, up, uc, uc, x, poolw, pscale, convw, convb, lng, lnb, pw, wout, g, wgu, wdown)


def kernel(x, norm_mix_g, w_in, sb_q_g, sb_k_g, pool_w, pool_scale, conv_w, conv_b, conv_ln_g,
           conv_ln_b, conv_pw, w_out, norm_ffn_g, ffn_w_gu, ffn_w_down):
    batch, seq, d = x.shape
    assert d == D_MODEL and seq % TOKEN_TILE == 0 and seq % Q_BLOCK == 0
    xf = x.reshape(batch * seq, d)

    head_of = jnp.arange(MXU_TILE) // SB_HEAD_DIM
    headavg = jnp.where(head_of[:, None] == head_of[None, :], 1.0 / SB_HEAD_DIM, 0.0).astype(BF16)
    kidx = jnp.arange(K_BLOCK)
    tri = -(kidx[:, None] >= kidx[None, :]).astype(BF16)
    row = lambda v: v.reshape(1, -1).astype(F32)

    for l in range(DEPTH):
        poolw_bd = jax.scipy.linalg.block_diag(*[pool_w[l, g] for g in range(len(POOL_WINDOWS))])
        q, k, v, up, uc = _inproj(
            xf, row(norm_mix_g[l]), w_in[l].astype(BF16),
            row(jnp.tile(sb_q_g[l], SB_HEADS)), row(jnp.tile(sb_k_g[l], SB_HEADS)), headavg)
        a = _attention(_attention_skip_threshold(sb_q_g[l], sb_k_g[l]), q, k, v, tri, batch, seq)
        xf = _mix_ffn(a, up, uc, xf, poolw_bd.astype(BF16), row(pool_scale[l]),
                      conv_w[l].astype(F32), row(conv_b[l]), row(conv_ln_g[l]),
                      row(conv_ln_b[l]), conv_pw[l].astype(BF16), w_out[l].astype(BF16),
                      row(norm_ffn_g[l]), ffn_w_gu[l].astype(BF16), ffn_w_down[l].astype(BF16),
                      seq)
    return xf.reshape(batch, seq, d)
```
